```python
import jax, jax.numpy as jnp
from jax import lax
import numpy as np

D_MODEL = 1024
BATCH = 16
SEQ = 4096
DEPTH = 1
DEC_BATCH = 4
DEC_SEQ = 4096
PAST_LEN = 128

N_HEADS = 8
QK_NOPE = 64
QK_ROPE = 32
QK_HEAD = QK_NOPE + QK_ROPE
V_HEAD = 64
Q_LORA = 384
KV_LORA = 256
ROPE_THETA = 10000.0
ATTN_BLOCK = 128
FNET_GROUPS = 4
FNET_GROUP_W = 128
FNET_W = FNET_GROUPS * FNET_GROUP_W
N_BRANCH = 2
OFF_Q = 0
OFF_KV = OFF_Q + Q_LORA
OFF_KR = OFF_KV + KV_LORA
OFF_F = OFF_KR + QK_ROPE
OFF_G = OFF_F + FNET_W
IN_W = OFF_G + N_BRANCH * D_MODEL
N_EXPERTS = 64
TOP_K = 8
N_EXPERT_GROUPS = 8
TOPK_GROUPS = 4
EXPERT_FF = 256
SHARED_FF = 256
ROUTED_SCALE = 2.5
MOE_BLOCK = 512
PLE_DIM = 256
EPS = 1e-6

kernel_name = 'hybrid_mla_fnet_moe_encoder'


def rms_norm(x, g):
    xf = x.astype(jnp.float32)
    y = xf * lax.rsqrt(jnp.mean(xf * xf, axis=-1, keepdims=True) + EPS)
    return (y * g.astype(jnp.float32)).astype(x.dtype)


def rope(x):
    S = x.shape[1]
    half = QK_ROPE // 2
    freqs = 1.0 / (ROPE_THETA ** (jnp.arange(half, dtype=jnp.float32) / half))
    ang = jnp.arange(S, dtype=jnp.float32)[:, None] * freqs[None, :]
    cos = jnp.cos(ang)[None, :, None, :]
    sin = jnp.sin(ang)[None, :, None, :]
    xf = x.astype(jnp.float32)
    x1, x2 = xf[..., :half], xf[..., half:]
    return jnp.concatenate([x1 * cos - x2 * sin, x2 * cos + x1 * sin], axis=-1).astype(x.dtype)


def dense_attention(q, k, v):
    B, S, H, _ = q.shape
    nb = S // ATTN_BLOCK
    scale = QK_HEAD ** -0.5
    qb = q.reshape(B, nb, ATTN_BLOCK, H, QK_HEAD).transpose(1, 0, 2, 3, 4)

    def one_block(qblk):
        s = jnp.einsum('bqhd,bkhd->bhqk', qblk, k, preferred_element_type=jnp.float32) * scale
        p = jax.nn.softmax(s, axis=-1)
        o = jnp.einsum('bhqk,bkhd->bqhd', p.astype(v.dtype), v, preferred_element_type=jnp.float32)
        return o.astype(v.dtype)

    o = lax.map(one_block, qb)
    return o.transpose(1, 0, 2, 3, 4).reshape(B, S, H * V_HEAD)


def fourier_mix(f):
    B, S, _ = f.shape
    fg = f.astype(jnp.float32).reshape(B, S, FNET_GROUPS, FNET_GROUP_W)
    out = jnp.fft.fft2(fg, axes=(1, 3), norm='ortho').real
    return out.reshape(B, S, FNET_W).astype(f.dtype)


def swiglu(x, wg, wu, wd):
    return (jax.nn.silu(x @ wg) * (x @ wu)) @ wd


def moe(h, w_router, b_router, w_e_gate, w_e_up, w_e_down, w_s_gate, w_s_up, w_s_down):
    B, S, D = h.shape
    T = B * S
    t = h.reshape(T, D)
    scores = jax.nn.sigmoid(jnp.matmul(t.astype(jnp.float32), w_router.astype(jnp.float32)))
    sel = scores + b_router.astype(jnp.float32)
    per_group = N_EXPERTS // N_EXPERT_GROUPS
    grp_score = lax.top_k(sel.reshape(T, N_EXPERT_GROUPS, per_group), 2)[0].sum(-1)
    _, top_g = lax.top_k(grp_score, TOPK_GROUPS)
    gmask = jax.nn.one_hot(top_g, N_EXPERT_GROUPS, dtype=jnp.float32).sum(1) > 0
    sel = jnp.where(jnp.repeat(gmask, per_group, axis=1), sel, -jnp.inf)
    _, idx = lax.top_k(sel, TOP_K)
    w = jnp.take_along_axis(scores, idx, axis=1)
    w = w / jnp.sum(w, axis=-1, keepdims=True) * ROUTED_SCALE
    A = T * TOP_K
    e_flat = idx.reshape(A).astype(jnp.int32)
    tok_flat = jnp.repeat(jnp.arange(T, dtype=jnp.int32), TOP_K)
    w_flat = w.reshape(A)
    order = jnp.argsort(e_flat, stable=True)
    e_s, tok_s, w_s = e_flat[order], tok_flat[order], w_flat[order]
    counts = jnp.bincount(e_flat, length=N_EXPERTS).astype(jnp.int32)
    padded = (counts + MOE_BLOCK - 1) // MOE_BLOCK * MOE_BLOCK
    pad_end = jnp.cumsum(padded)
    pad_start = pad_end - padded
    start = jnp.cumsum(counts) - counts
    dest = pad_start[e_s] + jnp.arange(A, dtype=jnp.int32) - start[e_s]
    NB = -(-A // MOE_BLOCK) + N_EXPERTS
    P = NB * MOE_BLOCK
    row_tok = jnp.zeros((P,), jnp.int32).at[dest].set(tok_s)
    row_w = jnp.zeros((P,), jnp.float32).at[dest].set(w_s)
    block_e = jnp.clip(jnp.searchsorted(pad_end, jnp.arange(NB, dtype=jnp.int32) * MOE_BLOCK, side='right'), 0, N_EXPERTS - 1)

    def expert_block(acc, blk):
        tok_b, w_b, e_b = blk
        xb = t[tok_b]
        yb = swiglu(xb, w_e_gate[e_b], w_e_up[e_b], w_e_down[e_b])
        acc = acc.at[tok_b].add((yb.astype(jnp.float32) * w_b[:, None]).astype(acc.dtype))
        return acc, None

    routed, _ = lax.scan(expert_block, jnp.zeros_like(t),
                         (row_tok.reshape(NB, MOE_BLOCK), row_w.reshape(NB, MOE_BLOCK), block_e))
    shared = swiglu(t, w_s_gate, w_s_up, w_s_down)
    return (routed + shared).reshape(B, S, D)


def block(x, p, g_mix, w_in, b_gate, g_qlat, w_q_up, g_kvlat, w_kv_up, g_qn, g_kn,
          w_a, w_f, w_o, g_ffn, w_router, b_router, w_e_gate, w_e_up, w_e_down,
          w_s_gate, w_s_up, w_s_down, g_ple, w_ple_gate, w_ple_proj):
    B, S, D = x.shape
    h = rms_norm(x, g_mix)
    proj = h @ w_in
    q_lat = rms_norm(proj[..., OFF_Q:OFF_KV], g_qlat)
    kv_lat = rms_norm(proj[..., OFF_KV:OFF_KR], g_kvlat)
    k_rope = proj[..., OFF_KR:OFF_F]
    f_in = proj[..., OFF_F:OFF_G]
    gates = jax.nn.sigmoid(proj[..., OFF_G:] + b_gate)
    q = (q_lat @ w_q_up).reshape(B, S, N_HEADS, QK_HEAD)
    kv = (kv_lat @ w_kv_up).reshape(B, S, N_HEADS, QK_NOPE + V_HEAD)
    k_nope, v = kv[..., :QK_NOPE], kv[..., QK_NOPE:]
    k = jnp.concatenate([k_nope, jnp.broadcast_to(k_rope[:, :, None, :], (B, S, N_HEADS, QK_ROPE))], axis=-1)
    q = rms_norm(q, g_qn)
    k = rms_norm(k, g_kn)
    q = jnp.concatenate([q[..., :QK_NOPE], rope(q[..., QK_NOPE:])], axis=-1)
    k = jnp.concatenate([k[..., :QK_NOPE], rope(k[..., QK_NOPE:])], axis=-1)
    a = dense_attention(q, k, v) @ w_a
    b = fourier_mix(f_in) @ w_f
    m = gates[..., :D] * a + gates[..., D:] * b
    x = x + m @ w_o
    x = x + moe(rms_norm(x, g_ffn), w_router, b_router, w_e_gate, w_e_up, w_e_down, w_s_gate, w_s_up, w_s_down)
    x = x + jax.nn.sigmoid(rms_norm(x, g_ple) @ w_ple_gate) * (p @ w_ple_proj)
    return x


def setup_inputs(seed: int = 0) -> dict:
    key = jax.random.key(seed)
    ks = jax.random.split(key, 32)
    f32 = jnp.float32

    def nrm(k, shape, fan_in):
        return jax.random.normal(k, shape, f32) * (fan_in ** -0.5)

    def gain(k, n):
        return jnp.ones((DEPTH, n), f32) + 0.02 * jax.random.normal(k, (DEPTH, n), f32)

    L = DEPTH
    return {
        'x_prompt': jax.random.normal(ks[0], (BATCH, SEQ, D_MODEL), f32),
        'x_sample': jax.random.normal(ks[1], (DEC_BATCH, DEC_SEQ, D_MODEL), f32),
        'p_prompt': jax.random.normal(ks[2], (DEPTH, BATCH, SEQ, PLE_DIM), f32),
        'p_sample': jax.random.normal(ks[3], (DEPTH, DEC_BATCH, DEC_SEQ, PLE_DIM), f32),
        'g_mix': gain(ks[4], D_MODEL),
        'w_in': nrm(ks[5], (L, D_MODEL, IN_W), D_MODEL),
        'b_gate': 0.02 * jax.random.normal(ks[6], (L, N_BRANCH * D_MODEL), f32),
        'g_qlat': gain(ks[7], Q_LORA),
        'w_q_up': nrm(ks[8], (L, Q_LORA, N_HEADS * QK_HEAD), Q_LORA),
        'g_kvlat': gain(ks[9], KV_LORA),
        'w_kv_up': nrm(ks[10], (L, KV_LORA, N_HEADS * (QK_NOPE + V_HEAD)), KV_LORA),
        'g_qn': gain(ks[11], QK_HEAD),
        'g_kn': gain(ks[12], QK_HEAD),
        'w_a': nrm(ks[13], (L, N_HEADS * V_HEAD, D_MODEL), N_HEADS * V_HEAD),
        'w_f': nrm(ks[14], (L, FNET_W, D_MODEL), FNET_W),
        'w_o': nrm(ks[15], (L, D_MODEL, D_MODEL), D_MODEL),
        'g_ffn': gain(ks[16], D_MODEL),
        'w_router': nrm(ks[17], (L, D_MODEL, N_EXPERTS), D_MODEL),
        'b_router': 0.01 * jax.random.normal(ks[18], (L, N_EXPERTS), f32),
        'w_e_gate': nrm(ks[19], (L, N_EXPERTS, D_MODEL, EXPERT_FF), D_MODEL),
        'w_e_up': nrm(ks[20], (L, N_EXPERTS, D_MODEL, EXPERT_FF), D_MODEL),
        'w_e_down': nrm(ks[21], (L, N_EXPERTS, EXPERT_FF, D_MODEL), EXPERT_FF),
        'w_s_gate': nrm(ks[22], (L, D_MODEL, SHARED_FF), D_MODEL),
        'w_s_up': nrm(ks[23], (L, D_MODEL, SHARED_FF), D_MODEL),
        'w_s_down': nrm(ks[24], (L, SHARED_FF, D_MODEL), SHARED_FF),
        'g_ple': gain(ks[25], D_MODEL),
        'w_ple_gate': nrm(ks[26], (L, D_MODEL, D_MODEL), D_MODEL),
        'w_ple_proj': nrm(ks[27], (L, PLE_DIM, D_MODEL), PLE_DIM),
    }


def reference(x_prompt, x_sample, p_prompt, p_sample, g_mix, w_in, b_gate, g_qlat, w_q_up,
              g_kvlat, w_kv_up, g_qn, g_kn, w_a, w_f, w_o, g_ffn, w_router, b_router,
              w_e_gate, w_e_up, w_e_down, w_s_gate, w_s_up, w_s_down, g_ple, w_ple_gate,
              w_ple_proj):
    xp = x_prompt
    xs = x_sample
    for l in range(DEPTH):
        lw = (g_mix[l], w_in[l], b_gate[l], g_qlat[l], w_q_up[l], g_kvlat[l], w_kv_up[l],
              g_qn[l], g_kn[l], w_a[l], w_f[l], w_o[l], g_ffn[l], w_router[l], b_router[l],
              w_e_gate[l], w_e_up[l], w_e_down[l], w_s_gate[l], w_s_up[l], w_s_down[l],
              g_ple[l], w_ple_gate[l], w_ple_proj[l])
        xp = block(xp, p_prompt[l], *lw)
        xs = block(xs, p_sample[l], *lw)
    return (xp, xs)
```

```python
import functools

import jax
import jax.numpy as jnp
import numpy as np
from jax import lax
from jax.experimental import pallas as pl
from jax.experimental.pallas import tpu as pltpu

D_MODEL = 1024
N_HEADS = 8
QK_NOPE = 64
QK_ROPE = 32
QK_HEAD = QK_NOPE + QK_ROPE
V_HEAD = 64
Q_LORA = 384
KV_LORA = 256
ROPE_THETA = 10000.0
FNET_GROUPS = 4
FNET_GROUP_W = 128
FNET_W = FNET_GROUPS * FNET_GROUP_W
N_EXPERTS = 64
TOP_K = 8
N_EXPERT_GROUPS = 8
TOPK_GROUPS = 4
EXPERT_FF = 256
SHARED_FF = 256
ROUTED_SCALE = 2.5
PLE_DIM = 256
EPS = 1e-6

LANES = 128
SUBLANES = 8
HEAD_PAD = LANES
ROWS_PER_TOKEN = D_MODEL // LANES

C_Q = 0
C_KV = C_Q + Q_LORA
C_KR = C_KV + KV_LORA
C_F = C_KR + LANES
C_G = C_F + FNET_W
C_END = C_G + 2 * D_MODEL

VMEM_LIMIT = 56 * 1024 * 1024
BF16 = jnp.bfloat16
F32 = jnp.float32


def _rms(x, n):
    return lax.rsqrt(jnp.sum(x * x, axis=-1, keepdims=True) * (1.0 / n) + EPS)


def _dot(a, b):
    return jnp.dot(a, b, preferred_element_type=F32)


def _inproj_kernel(x_ref, cos_ref, sin_ref, gmix_ref, w1_ref, bg_ref, gq_ref, gkv_ref,
                   wq_ref, wqr_ref, wk_ref, wv_ref, ekr_ref, gqc_ref, gqs_ref, gkc_ref,
                   gks_ref, bd_ref, q_out, k_out, v_out, f_out, g_out):
    x = x_ref[...]
    h = (x * _rms(x, D_MODEL) * gmix_ref[...]).astype(BF16)
    proj = _dot(h, w1_ref[...])
    q_lat = proj[:, C_Q:C_KV]
    q_lat = (q_lat * _rms(q_lat, Q_LORA) * gq_ref[...]).astype(BF16)
    kv_lat = proj[:, C_KV:C_KR]
    kv_lat = (kv_lat * _rms(kv_lat, KV_LORA) * gkv_ref[...]).astype(BF16)
    kr = proj[:, C_KR:C_F].astype(BF16)
    f_in = proj[:, C_F:C_G].astype(BF16)
    g_out[...] = jax.nn.sigmoid(proj[:, C_G:C_END] + bg_ref[...]).astype(BF16)
    f_out[...] = _dot(f_in, bd_ref[...]).astype(BF16)
    v_out[...] = _dot(kv_lat, wv_ref[...]).astype(BF16)

    q = _dot(q_lat, wq_ref[...])
    qr = _dot(q_lat, wqr_ref[...])
    kre = _dot(kr, ekr_ref[...])
    k = _dot(kv_lat, wk_ref[...]) + kre[:, :N_HEADS * HEAD_PAD]
    kr_rot = kre[:, N_HEADS * HEAD_PAD:]
    cos = cos_ref[...]
    sin = sin_ref[...]
    qc = gqc_ref[...] * cos
    qs = gqs_ref[...] * sin
    kc = gkc_ref[...] * cos
    ks = gks_ref[...] * sin
    scale = QK_HEAD ** -0.5
    for hd in range(N_HEADS):
        sl = slice(hd * HEAD_PAD, (hd + 1) * HEAD_PAD)
        qh = q[:, sl]
        rq = _rms(qh, QK_HEAD) * scale
        q_out[:, sl] = ((qh * qc + qr[:, sl] * qs) * rq).astype(BF16)
        kh = k[:, sl]
        rk = _rms(kh, QK_HEAD)
        k_out[:, sl] = ((kh * kc + kr_rot[:, sl] * ks) * rk).astype(BF16)


def _inproj(x2d, S, tabs, wts, tm):
    T = x2d.shape[0]
    n_pos = S // tm
    full = lambda a: pl.BlockSpec(a.shape, lambda i: (0,) * a.ndim)
    tok = lambda w: pl.BlockSpec((tm, w), lambda i: (i, 0))
    pos = pl.BlockSpec((tm, HEAD_PAD), lambda i: (i % n_pos, 0))
    names = ['g_mix', 'w1', 'b_gate', 'g_qlat', 'g_kvlat', 'wq', 'wqr', 'wk', 'wv', 'ekr',
             'gqc', 'gqs', 'gkc', 'gks', 'bd']
    ws = [wts[n] for n in names]
    return pl.pallas_call(
        _inproj_kernel,
        out_shape=(jax.ShapeDtypeStruct((T, N_HEADS * HEAD_PAD), BF16),
                   jax.ShapeDtypeStruct((T, N_HEADS * HEAD_PAD), BF16),
                   jax.ShapeDtypeStruct((T, N_HEADS * V_HEAD), BF16),
                   jax.ShapeDtypeStruct((T, 2 * FNET_W), BF16),
                   jax.ShapeDtypeStruct((T, 2 * D_MODEL), BF16)),
        grid=(T // tm,),
        in_specs=[tok(D_MODEL), pos, pos] + [full(w) for w in ws],
        out_specs=(tok(N_HEADS * HEAD_PAD), tok(N_HEADS * HEAD_PAD), tok(N_HEADS * V_HEAD),
                   tok(2 * FNET_W), tok(2 * D_MODEL)),
        compiler_params=pltpu.CompilerParams(dimension_semantics=("arbitrary",),
                                             vmem_limit_bytes=VMEM_LIMIT),
        name="inproj",
    )(x2d, tabs['cos'], tabs['sin'], *ws)


def _attn_kernel(q_ref, k_ref, v_ref, o_ref, *, tk):
    S = k_ref.shape[0]
    tq = q_ref.shape[0]
    outs = []
    for a in range(2):
        qa = q_ref[:, a * HEAD_PAD:(a + 1) * HEAD_PAD]

        def body(c, carry, a=a, qa=qa):
            m, l, acc = carry
            off = pl.multiple_of(c * tk, tk)
            kc = k_ref[pl.ds(off, tk), a * HEAD_PAD:(a + 1) * HEAD_PAD]
            s = lax.dot_general(qa, kc, (((1,), (1,)), ((), ())), preferred_element_type=F32)
            m_new = jnp.maximum(m, jnp.max(s, axis=-1, keepdims=True))
            alpha = jnp.exp(m - m_new)
            p = jnp.exp(s - m_new)
            l = alpha * l + jnp.sum(p, axis=-1, keepdims=True)
            acc = alpha * acc + _dot(p.astype(BF16), v_ref[pl.ds(off, tk), :])
            return m_new, l, acc

        m0 = jnp.full((tq, 1), -jnp.inf, F32)
        l0 = jnp.zeros((tq, 1), F32)
        acc0 = jnp.zeros((tq, 2 * V_HEAD), F32)
        m, l, acc = lax.fori_loop(0, S // tk, body, (m0, l0, acc0))
        outs.append(acc / l)
    lane = lax.broadcasted_iota(jnp.int32, (tq, 2 * V_HEAD), 1)
    o_ref[...] = jnp.where(lane < V_HEAD, outs[0], outs[1]).astype(o_ref.dtype)


def _attention(q, k, v, tq, tk):
    B, S, _ = q.shape
    return pl.pallas_call(
        functools.partial(_attn_kernel, tk=tk),
        out_shape=jax.ShapeDtypeStruct((B, S, N_HEADS * V_HEAD), BF16),
        grid=(B, N_HEADS // 2, S // tq),
        in_specs=[pl.BlockSpec((None, tq, 2 * HEAD_PAD), lambda b, p, i: (b, i, p)),
                  pl.BlockSpec((None, S, 2 * HEAD_PAD), lambda b, p, i: (b, 0, p)),
                  pl.BlockSpec((None, S, 2 * V_HEAD), lambda b, p, i: (b, 0, p))],
        out_specs=pl.BlockSpec((None, tq, 2 * V_HEAD), lambda b, p, i: (b, i, p)),
        compiler_params=pltpu.CompilerParams(
            dimension_semantics=("arbitrary", "arbitrary", "arbitrary"),
            vmem_limit_bytes=VMEM_LIMIT),
        name="attention",
    )(q, k, v)


def _seqdft_kernel(c_ref, s_ref, f_ref, o_ref):
    y = _dot(c_ref[...], f_ref[:, :FNET_W]) + _dot(s_ref[...], f_ref[:, FNET_W:])
    o_ref[...] = y.astype(o_ref.dtype)


def _seqdft(fcs, ctab, stab, tm):
    B, S, _ = fcs.shape
    return pl.pallas_call(
        _seqdft_kernel,
        out_shape=jax.ShapeDtypeStruct((B, S, FNET_W), BF16),
        grid=(S // tm, B),
        in_specs=[pl.BlockSpec((tm, S), lambda i, b: (i, 0)),
                  pl.BlockSpec((tm, S), lambda i, b: (i, 0)),
                  pl.BlockSpec((None, S, 2 * FNET_W), lambda i, b: (b, 0, 0))],
        out_specs=pl.BlockSpec((None, tm, FNET_W), lambda i, b: (b, i, 0)),
        compiler_params=pltpu.CompilerParams(dimension_semantics=("arbitrary", "arbitrary"),
                                             vmem_limit_bytes=VMEM_LIMIT),
        name="seqdft",
    )(ctab, stab, fcs)


def _first_argmax(v, iota, n):
    m = jnp.max(v, axis=0, keepdims=True)
    i = jnp.min(jnp.where(v == m, iota, n), axis=0, keepdims=True)
    return m, i


def _route(logits_t, bias):
    tm = logits_t.shape[1]
    per = N_EXPERTS // N_EXPERT_GROUPS
    scores = jax.nn.sigmoid(logits_t)
    sel = scores + bias
    neg = jnp.float32(-jnp.inf)
    iota_p = lax.broadcasted_iota(jnp.int32, (per, tm), 0)
    iota_g = lax.broadcasted_iota(jnp.int32, (N_EXPERT_GROUPS, tm), 0)
    gscore = jnp.zeros((N_EXPERT_GROUPS, tm), F32)
    for g in range(N_EXPERT_GROUPS):
        v = sel[g * per:(g + 1) * per, :]
        m1, i1 = _first_argmax(v, iota_p, per)
        m2 = jnp.max(jnp.where(iota_p == i1, neg, v), axis=0, keepdims=True)
        gscore = jnp.where(iota_g == g, m1 + m2, gscore)
    gmask = jnp.zeros((N_EXPERT_GROUPS, tm), jnp.bool_)
    v = gscore
    for _ in range(TOPK_GROUPS):
        _, i = _first_argmax(v, iota_g, N_EXPERT_GROUPS)
        hit = iota_g == i
        gmask = jnp.logical_or(gmask, hit)
        v = jnp.where(hit, neg, v)
    gkeep = jnp.where(gmask, 1.0, 0.0)
    parts = []
    for g in range(N_EXPERT_GROUPS):
        keep = gkeep[g:g + 1, :] > 0.5
        parts.append(jnp.where(keep, sel[g * per:(g + 1) * per, :], neg))
    v = jnp.concatenate(parts, axis=0)
    iota_e = lax.broadcasted_iota(jnp.int32, (N_EXPERTS, tm), 0)
    iota_k = lax.broadcasted_iota(jnp.int32, (TOP_K, tm), 0)
    idx = jnp.zeros((TOP_K, tm), jnp.int32)
    wts = jnp.zeros((TOP_K, tm), F32)
    for kk in range(TOP_K):
        _, i = _first_argmax(v, iota_e, N_EXPERTS)
        hit = iota_e == i
        wk = jnp.sum(jnp.where(hit, scores, 0.0), axis=0, keepdims=True)
        v = jnp.where(hit, neg, v)
        idx = jnp.where(iota_k == kk, i, idx)
        wts = jnp.where(iota_k == kk, wk, wts)
    wts = wts / jnp.sum(wts, axis=0, keepdims=True) * ROUTED_SCALE
    return idx, wts


def _merge_kernel(x_ref, o_ref, bf_ref, g_ref, wa_ref, wf_ref, wo_ref, gffn_ref, wsgu_ref,
                  wsd_ref, wrt_ref, br_ref, hn_out, base_out, idx_out, w_out):
    a = _dot(o_ref[...], wa_ref[...])
    b = _dot(bf_ref[...], wf_ref[...])
    g = g_ref[...].astype(F32)
    m = (g[:, :D_MODEL] * a + g[:, D_MODEL:] * b).astype(BF16)
    x2 = x_ref[...] + _dot(m, wo_ref[...])
    hn = x2 * _rms(x2, D_MODEL) * gffn_ref[...]
    hn_out[...] = hn
    hb = hn.astype(BF16)
    gu = _dot(hb, wsgu_ref[...])
    act = (jax.nn.silu(gu[:, :SHARED_FF]) * gu[:, SHARED_FF:]).astype(BF16)
    base_out[...] = x2 + _dot(act, wsd_ref[...])
    logits_t = lax.dot_general(wrt_ref[...], hn, (((1,), (1,)), ((), ())),
                               precision=lax.Precision.HIGHEST, preferred_element_type=F32)
    idx, wts = _route(logits_t, br_ref[...])
    idx_out[...] = idx
    w_out[...] = wts


def _merge(x2d, o2d, bf2d, gates, wts, tm):
    T = x2d.shape[0]
    full = lambda a: pl.BlockSpec(a.shape, lambda i: (0,) * a.ndim)
    tok = lambda w: pl.BlockSpec((tm, w), lambda i: (i, 0))
    names = ['w_a', 'w_f', 'w_o', 'g_ffn', 'w_sgu', 'w_sd', 'w_rt', 'b_r']
    ws = [wts[n] for n in names]
    return pl.pallas_call(
        _merge_kernel,
        out_shape=(jax.ShapeDtypeStruct((T, D_MODEL), F32),
                   jax.ShapeDtypeStruct((T, D_MODEL), F32),
                   jax.ShapeDtypeStruct((TOP_K, T), jnp.int32),
                   jax.ShapeDtypeStruct((TOP_K, T), F32)),
        grid=(T // tm,),
        in_specs=[tok(D_MODEL), tok(N_HEADS * V_HEAD), tok(FNET_W), tok(2 * D_MODEL)]
                 + [full(w) for w in ws],
        out_specs=(tok(D_MODEL), tok(D_MODEL),
                   pl.BlockSpec((TOP_K, tm), lambda i: (0, i)),
                   pl.BlockSpec((TOP_K, tm), lambda i: (0, i))),
        compiler_params=pltpu.CompilerParams(dimension_semantics=("arbitrary",),
                                             vmem_limit_bytes=VMEM_LIMIT),
        name="merge",
    )(x2d, o2d, bf2d, gates, *ws)


def _moe_kernel(cnt_ref, start_ref, tok_hbm, w_hbm, hn_ref, wg_ref, wu_ref, wd_ref, out_ref,
                tok_s, w_s, gbuf, ybuf, sem, *, chunk):
    i = pl.program_id(0)
    e = pl.program_id(1)
    R = ROWS_PER_TOKEN

    @pl.when(jnp.logical_and(i == 0, e == 0))
    def _():
        gbuf[...] = jnp.zeros_like(gbuf)

    @pl.when(e == 0)
    def _():
        cp_t = pltpu.make_async_copy(tok_hbm.at[i], tok_s, sem.at[0])
        cp_w = pltpu.make_async_copy(w_hbm.at[i], w_s, sem.at[1])
        cp_t.start()
        cp_w.start()
        out_ref[...] = jnp.zeros_like(out_ref)
        cp_t.wait()
        cp_w.wait()

    n = cnt_ref[i * N_EXPERTS + e]
    st = start_ref[i * N_EXPERTS + e]

    def do_chunk(c, _):
        base = st + c * chunk
        rows = jnp.minimum(chunk, n - c * chunk)

        def gather_row(j, _):
            t = tok_s[base + j]
            gbuf[pl.ds(pl.multiple_of(j * R, R), R), :] = hn_ref[pl.ds(pl.multiple_of(t * R, R), R), :]
            return 0

        lax.fori_loop(0, rows, gather_row, 0)
        xg = jnp.concatenate([gbuf[pl.ds(r, chunk, stride=R), :] for r in range(R)],
                             axis=1).astype(BF16)
        g = _dot(xg, wg_ref[0])
        u = _dot(xg, wu_ref[0])
        act = (jax.nn.silu(g) * u).astype(BF16)
        y = _dot(act, wd_ref[0])
        for r in range(R):
            ybuf[pl.ds(r, chunk, stride=R), :] = y[:, r * LANES:(r + 1) * LANES]

        def scatter_row(j, _):
            t = tok_s[base + j]
            w = w_s[base + j]
            dst = pl.ds(pl.multiple_of(t * R, R), R)
            out_ref[dst, :] = out_ref[dst, :] + w * ybuf[pl.ds(pl.multiple_of(j * R, R), R), :]
            return 0

        lax.fori_loop(0, rows, scatter_row, 0)
        return 0

    lax.fori_loop(0, pl.cdiv(n, chunk), do_chunk, 0)


def _moe(hn2d, cnt, start, tok_sorted, w_sorted, wts, tmoe, chunk):
    T = hn2d.shape[0]
    n_tiles = T // tmoe
    L = tok_sorted.shape[1]
    R = ROWS_PER_TOKEN
    hn_tiles = hn2d.reshape(T * R, LANES)
    wspec = lambda a: pl.BlockSpec((1,) + a.shape[1:], lambda i, e, *_: (e, 0, 0))
    out = pl.pallas_call(
        functools.partial(_moe_kernel, chunk=chunk),
        out_shape=jax.ShapeDtypeStruct((T * R, LANES), F32),
        grid_spec=pltpu.PrefetchScalarGridSpec(
            num_scalar_prefetch=2,
            grid=(n_tiles, N_EXPERTS),
            in_specs=[pl.BlockSpec(memory_space=pl.ANY),
                      pl.BlockSpec(memory_space=pl.ANY),
                      pl.BlockSpec((tmoe * R, LANES), lambda i, e, *_: (i, 0)),
                      wspec(wts['w_eg']), wspec(wts['w_eu']), wspec(wts['w_ed'])],
            out_specs=pl.BlockSpec((tmoe * R, LANES), lambda i, e, *_: (i, 0)),
            scratch_shapes=[pltpu.SMEM((L,), jnp.int32),
                            pltpu.SMEM((L,), F32),
                            pltpu.VMEM((chunk * R, LANES), F32),
                            pltpu.VMEM((chunk * R, LANES), F32),
                            pltpu.SemaphoreType.DMA((2,))]),
        compiler_params=pltpu.CompilerParams(dimension_semantics=("arbitrary", "arbitrary"),
                                             vmem_limit_bytes=VMEM_LIMIT),
        name="moe",
    )(cnt, start, tok_sorted, w_sorted, hn_tiles, wts['w_eg'], wts['w_eu'], wts['w_ed'])
    return out.reshape(T, D_MODEL)


def _dispatch(idx, w, tmoe):
    T = idx.shape[1]
    n_tiles = T // tmoe
    tok = jnp.arange(T, dtype=jnp.int32)
    key = (tok // tmoe)[None, :] * N_EXPERTS + idx
    tok_l = jnp.broadcast_to((tok % tmoe)[None, :], idx.shape)
    _, tok_sorted, w_sorted = lax.sort((key.reshape(-1), tok_l.reshape(-1), w.reshape(-1)),
                                       num_keys=1)
    onehot = idx.reshape(TOP_K, n_tiles, tmoe)[..., None] == jnp.arange(N_EXPERTS, dtype=jnp.int32)
    cnt = jnp.sum(onehot.astype(jnp.int32), axis=(0, 2))
    start = jnp.cumsum(cnt, axis=1) - cnt
    tok_sorted = tok_sorted.reshape(n_tiles, TOP_K * tmoe)
    w_sorted = w_sorted.reshape(n_tiles, TOP_K * tmoe)
    return cnt.reshape(-1), start.reshape(-1), tok_sorted, w_sorted


def _ple_kernel(base_ref, routed_ref, p_ref, gple_ref, wg_ref, wp_ref, y_ref):
    x3 = base_ref[...] + routed_ref[...]
    hn = (x3 * _rms(x3, D_MODEL) * gple_ref[...]).astype(BF16)
    gate = jax.nn.sigmoid(_dot(hn, wg_ref[...]))
    y_ref[...] = x3 + gate * _dot(p_ref[...].astype(BF16), wp_ref[...])


def _ple(base, routed, p2d, wts, tm):
    T = base.shape[0]
    full = lambda a: pl.BlockSpec(a.shape, lambda i: (0,) * a.ndim)
    tok = lambda w: pl.BlockSpec((tm, w), lambda i: (i, 0))
    ws = [wts['g_ple'], wts['w_pg'], wts['w_pp']]
    return pl.pallas_call(
        _ple_kernel,
        out_shape=jax.ShapeDtypeStruct((T, D_MODEL), F32),
        grid=(T // tm,),
        in_specs=[tok(D_MODEL), tok(D_MODEL), tok(PLE_DIM)] + [full(w) for w in ws],
        out_specs=tok(D_MODEL),
        compiler_params=pltpu.CompilerParams(dimension_semantics=("arbitrary",),
                                             vmem_limit_bytes=VMEM_LIMIT),
        name="ple",
    )(base, routed, p2d, *ws)


def _head_pad_cols(w, width):
    K = w.shape[0]
    w = w.reshape(K, N_HEADS, width)
    return jnp.pad(w, ((0, 0), (0, 0), (0, HEAD_PAD - width))).reshape(K, N_HEADS * HEAD_PAD)


def _rot_partner(w):
    half = QK_ROPE // 2
    z = jnp.zeros_like(w[..., :QK_NOPE])
    return jnp.concatenate([z, -w[..., QK_NOPE + half:QK_HEAD], w[..., QK_NOPE:QK_NOPE + half]], axis=-1)


def _prep_weights(g_mix, w_in, b_gate, g_qlat, w_q_up, g_kvlat, w_kv_up, g_qn, g_kn, w_a, w_f,
                  w_o, g_ffn, w_router, b_router, w_e_gate, w_e_up, w_e_down, w_s_gate, w_s_up,
                  w_s_down, g_ple, w_ple_gate, w_ple_proj):
    half = QK_ROPE // 2
    off_kv, off_kr = Q_LORA, Q_LORA + KV_LORA
    off_f = off_kr + QK_ROPE
    off_g = off_f + FNET_W
    w1 = jnp.concatenate([
        w_in[:, :off_kr],
        jnp.pad(w_in[:, off_kr:off_f], ((0, 0), (0, LANES - QK_ROPE))),
        w_in[:, off_f:]], axis=1).astype(BF16)
    wq3 = w_q_up.reshape(Q_LORA, N_HEADS, QK_HEAD)
    wq = _head_pad_cols(w_q_up, QK_HEAD).astype(BF16)
    wqr = _head_pad_cols(_rot_partner(wq3).reshape(Q_LORA, -1), QK_HEAD).astype(BF16)
    wkv3 = w_kv_up.reshape(KV_LORA, N_HEADS, QK_NOPE + V_HEAD)
    wk = _head_pad_cols(wkv3[:, :, :QK_NOPE].reshape(KV_LORA, -1), QK_NOPE).astype(BF16)
    wv = wkv3[:, :, QK_NOPE:].reshape(KV_LORA, N_HEADS * V_HEAD).astype(BF16)
    eye = jnp.eye(QK_ROPE, dtype=F32)
    place = jnp.concatenate([jnp.zeros((QK_ROPE, QK_NOPE), F32), eye], axis=1)
    place_rot = _rot_partner(place)
    ekr = jnp.concatenate([
        _head_pad_cols(jnp.tile(place, (1, N_HEADS)), QK_HEAD),
        _head_pad_cols(jnp.tile(place_rot, (1, N_HEADS)), QK_HEAD)], axis=1)
    ekr = jnp.pad(ekr, ((0, LANES - QK_ROPE), (0, 0))).astype(BF16)

    def gain_tabs(g):
        one_nope = jnp.concatenate([jnp.ones((QK_NOPE,), F32), jnp.zeros((QK_ROPE,), F32)])
        gp = jnp.concatenate([jnp.zeros((QK_NOPE,), F32), g[QK_NOPE + half:], g[QK_NOPE:QK_NOPE + half]])
        pad = lambda a: jnp.pad(a, (0, HEAD_PAD - QK_HEAD)).reshape(1, HEAD_PAD)
        del one_nope
        return pad(g), pad(gp)

    gqc, gqs = gain_tabs(g_qn)
    gkc, gks = gain_tabs(g_kn)
    j = jnp.arange(FNET_GROUP_W, dtype=jnp.int32)
    ang = (2.0 * np.pi / FNET_GROUP_W) * ((j[:, None] * j[None, :]) % FNET_GROUP_W).astype(F32)
    eye_g = jnp.eye(FNET_GROUPS, dtype=F32)
    bd = jnp.concatenate([jnp.kron(eye_g, jnp.cos(ang)), jnp.kron(eye_g, jnp.sin(ang))], axis=1).astype(BF16)
    row = lambda a: a.reshape(1, -1).astype(F32)
    return {
        'g_mix': row(g_mix), 'w1': w1, 'b_gate': row(b_gate), 'g_qlat': row(g_qlat),
        'g_kvlat': row(g_kvlat), 'wq': wq, 'wqr': wqr, 'wk': wk, 'wv': wv, 'ekr': ekr,
        'gqc': gqc, 'gqs': gqs, 'gkc': gkc, 'gks': gks, 'bd': bd,
        'w_a': w_a.astype(BF16), 'w_f': w_f.astype(BF16), 'w_o': w_o.astype(BF16),
        'g_ffn': row(g_ffn),
        'w_sgu': jnp.concatenate([w_s_gate, w_s_up], axis=1).astype(BF16),
        'w_sd': w_s_down.astype(BF16),
        'w_rt': w_router.T.astype(F32), 'b_r': b_router.reshape(N_EXPERTS, 1).astype(F32),
        'w_eg': w_e_gate.astype(BF16), 'w_eu': w_e_up.astype(BF16), 'w_ed': w_e_down.astype(BF16),
        'g_ple': row(g_ple), 'w_pg': w_ple_gate.astype(BF16), 'w_pp': w_ple_proj.astype(BF16),
    }


def _tables(S):
    half = QK_ROPE // 2
    freqs = 1.0 / (ROPE_THETA ** (jnp.arange(half, dtype=F32) / half))
    ang = jnp.arange(S, dtype=F32)[:, None] * freqs[None, :]
    cos, sin = jnp.cos(ang), jnp.sin(ang)
    cos_t = jnp.concatenate([jnp.ones((S, QK_NOPE), F32), cos, cos,
                             jnp.zeros((S, HEAD_PAD - QK_HEAD), F32)], axis=1)
    sin_t = jnp.concatenate([jnp.zeros((S, QK_NOPE), F32), sin, sin,
                             jnp.zeros((S, HEAD_PAD - QK_HEAD), F32)], axis=1)
    s = jnp.arange(S, dtype=jnp.int32)
    a = (2.0 * np.pi / S) * ((s[:, None] * s[None, :]) % S).astype(F32)
    norm = (S * FNET_GROUP_W) ** -0.5
    return {'cos': cos_t, 'sin': sin_t,
            'dft_c': (jnp.cos(a) * norm).astype(BF16),
            'dft_s': (-jnp.sin(a) * norm).astype(BF16)}


def _block(x, p, wts, tabs, *, tm, tq, tk, tdft, tmoe, chunk):
    B, S, D = x.shape
    T = B * S
    x2d = x.reshape(T, D)
    q, k, v, fcs, gates = _inproj(x2d, S, tabs, wts, tm)
    o = _attention(q.reshape(B, S, -1), k.reshape(B, S, -1), v.reshape(B, S, -1), tq, tk)
    bf = _seqdft(fcs.reshape(B, S, -1), tabs['dft_c'], tabs['dft_s'], tdft)
    hn, base, idx, w = _merge(x2d, o.reshape(T, -1), bf.reshape(T, -1), gates, wts, tm)
    tmoe = min(tmoe, T)
    cnt, start, tok_sorted, w_sorted = _dispatch(idx, w, tmoe)
    routed = _moe(hn, cnt, start, tok_sorted, w_sorted, wts, tmoe, chunk)
    y = _ple(base, routed, p.reshape(T, -1), wts, tm)
    return y.reshape(B, S, D)


def kernel(x_prompt, x_sample, p_prompt, p_sample, g_mix, w_in, b_gate, g_qlat, w_q_up, g_kvlat, w_kv_up, g_qn, g_kn, w_a, w_f, w_o, g_ffn, w_router, b_router, w_e_gate, w_e_up, w_e_down, w_s_gate, w_s_up, w_s_down, g_ple, w_ple_gate, w_ple_proj):
    params = (g_mix, w_in, b_gate, g_qlat, w_q_up, g_kvlat, w_kv_up, g_qn, g_kn, w_a, w_f, w_o,
              g_ffn, w_router, b_router, w_e_gate, w_e_up, w_e_down, w_s_gate, w_s_up, w_s_down,
              g_ple, w_ple_gate, w_ple_proj)
    depth = g_mix.shape[0]
    cfg = dict(tm=512, tq=256, tk=512, tdft=512, tmoe=2048, chunk=128)
    xp, xs = x_prompt, x_sample
    tabs = _tables(x_prompt.shape[1])
    tabs_s = tabs if x_sample.shape[1] == x_prompt.shape[1] else _tables(x_sample.shape[1])
    for l in range(depth):
        wts = _prep_weights(*[a[l] for a in params])
        xp = _block(xp, p_prompt[l], wts, tabs, **cfg)
        xs = _block(xs, p_sample[l], wts, tabs_s, **cfg)
    return (xp, xs)
```

```python
import functools

import jax
import jax.numpy as jnp
import numpy as np
from jax import lax
from jax.experimental import pallas as pl
from jax.experimental.pallas import tpu as pltpu

D_MODEL = 1024
N_HEADS = 8
QK_NOPE = 64
QK_ROPE = 32
QK_HEAD = QK_NOPE + QK_ROPE
V_HEAD = 64
Q_LORA = 384
KV_LORA = 256
ROPE_THETA = 10000.0
FNET_GROUPS = 4
FNET_GROUP_W = 128
FNET_W = FNET_GROUPS * FNET_GROUP_W
N_EXPERTS = 64
TOP_K = 8
N_EXPERT_GROUPS = 8
TOPK_GROUPS = 4
EXPERT_FF = 256
SHARED_FF = 256
ROUTED_SCALE = 2.5
PLE_DIM = 256
EPS = 1e-6

LANES = 128
SUBLANES = 8
HEAD_PAD = LANES
ROWS_PER_TOKEN = D_MODEL // LANES
GROUP = 8

C_Q = 0
C_KV = C_Q + Q_LORA
C_KR = C_KV + KV_LORA
C_F = C_KR + LANES
C_G = C_F + FNET_W
C_END = C_G + 2 * D_MODEL

VMEM_LIMIT = 56 * 1024 * 1024
BF16 = jnp.bfloat16
F32 = jnp.float32


def _rms(x, n):
    return lax.rsqrt(jnp.sum(x * x, axis=-1, keepdims=True) * (1.0 / n) + EPS)


def _dot(a, b):
    return jnp.dot(a, b, preferred_element_type=F32)


def _inproj_kernel(x_ref, cos_ref, sin_ref, gmix_ref, w1_ref, bg_ref, gq_ref, gkv_ref,
                   wq_ref, wqr_ref, wk_ref, wv_ref, vone_ref, ekr_ref, gqc_ref, gqs_ref, gkc_ref,
                   gks_ref, bd_ref, q_out, k_out, v_out, f_out, g_out):
    x = x_ref[...]
    h = (x * _rms(x, D_MODEL) * gmix_ref[...]).astype(BF16)
    proj = _dot(h, w1_ref[...])
    q_lat = proj[:, C_Q:C_KV]
    q_lat = (q_lat * _rms(q_lat, Q_LORA) * gq_ref[...]).astype(BF16)
    kv_lat = proj[:, C_KV:C_KR]
    kv_lat = (kv_lat * _rms(kv_lat, KV_LORA) * gkv_ref[...]).astype(BF16)
    kr = proj[:, C_KR:C_F].astype(BF16)
    f_in = proj[:, C_F:C_G].astype(BF16)
    g_out[...] = jax.nn.sigmoid(proj[:, C_G:C_END] + bg_ref[...]).astype(BF16)
    f_out[...] = _dot(f_in, bd_ref[...]).astype(BF16)
    v_out[...] = (_dot(kv_lat, wv_ref[...]) + vone_ref[...]).astype(BF16)

    q = _dot(q_lat, wq_ref[...])
    qr = _dot(q_lat, wqr_ref[...])
    kre = _dot(kr, ekr_ref[...])
    k = _dot(kv_lat, wk_ref[...]) + kre[:, :N_HEADS * HEAD_PAD]
    kr_rot = kre[:, N_HEADS * HEAD_PAD:]
    cos = cos_ref[...]
    sin = sin_ref[...]
    qc = gqc_ref[...] * cos
    qs = gqs_ref[...] * sin
    kc = gkc_ref[...] * cos
    ks = gks_ref[...] * sin
    scale = QK_HEAD ** -0.5 * np.log2(np.e)
    for hd in range(N_HEADS):
        sl = slice(hd * HEAD_PAD, (hd + 1) * HEAD_PAD)
        qh = q[:, sl]
        rq = _rms(qh, QK_HEAD) * scale
        q_out[:, sl] = ((qh * qc + qr[:, sl] * qs) * rq).astype(BF16)
        kh = k[:, sl]
        rk = _rms(kh, QK_HEAD)
        k_out[:, sl] = ((kh * kc + kr_rot[:, sl] * ks) * rk).astype(BF16)


def _inproj(x2d, S, tabs, wts, tm):
    T = x2d.shape[0]
    n_pos = S // tm
    full = lambda a: pl.BlockSpec(a.shape, lambda i: (0,) * a.ndim)
    tok = lambda w: pl.BlockSpec((tm, w), lambda i: (i, 0))
    pos = pl.BlockSpec((tm, HEAD_PAD), lambda i: (i % n_pos, 0))
    names = ['g_mix', 'w1', 'b_gate', 'g_qlat', 'g_kvlat', 'wq', 'wqr', 'wk', 'wv', 'vone', 'ekr',
             'gqc', 'gqs', 'gkc', 'gks', 'bd']
    ws = [wts[n] for n in names]
    return pl.pallas_call(
        _inproj_kernel,
        out_shape=(jax.ShapeDtypeStruct((T, N_HEADS * HEAD_PAD), BF16),
                   jax.ShapeDtypeStruct((T, N_HEADS * HEAD_PAD), BF16),
                   jax.ShapeDtypeStruct((T, N_HEADS * HEAD_PAD), BF16),
                   jax.ShapeDtypeStruct((T, 2 * FNET_W), BF16),
                   jax.ShapeDtypeStruct((T, 2 * D_MODEL), BF16)),
        grid=(T // tm,),
        in_specs=[tok(D_MODEL), pos, pos] + [full(w) for w in ws],
        out_specs=(tok(N_HEADS * HEAD_PAD), tok(N_HEADS * HEAD_PAD), tok(N_HEADS * HEAD_PAD),
                   tok(2 * FNET_W), tok(2 * D_MODEL)),
        compiler_params=pltpu.CompilerParams(dimension_semantics=("arbitrary",),
                                             vmem_limit_bytes=VMEM_LIMIT),
        name="inproj",
    )(x2d, tabs['cos'], tabs['sin'], *ws)


def _attn_kernel(q_ref, k_ref, v_ref, o_ref, *, tk):
    S = k_ref.shape[0]
    tq = q_ref.shape[0]
    nblk = tk // LANES

    def body(c, carry):
        off = pl.multiple_of(c * tk, tk)
        new = []
        for a in range(2):
            hs = slice(a * HEAD_PAD, (a + 1) * HEAD_PAD)
            m, acc = carry[a]
            s = lax.dot_general(q_ref[:, hs], k_ref[pl.ds(off, tk), hs],
                                (((1,), (1,)), ((), ())), preferred_element_type=F32)
            mx = s[:, :LANES]
            for j in range(1, nblk):
                mx = jnp.maximum(mx, s[:, j * LANES:(j + 1) * LANES])
            m_new = jnp.maximum(m, jnp.max(mx, axis=-1, keepdims=True))
            alpha = jnp.exp2(m - m_new)
            p = jnp.concatenate(
                [jnp.exp2((s[:, j * LANES:(j + 1) * LANES] - m_new).astype(BF16)) for j in range(nblk)],
                axis=1)
            acc = alpha * acc + _dot(p, v_ref[pl.ds(off, tk), hs])
            new.append((m_new, acc))
        return tuple(new)

    init = (jnp.full((tq, LANES), -jnp.inf, F32), jnp.zeros((tq, HEAD_PAD), F32))
    (_, acc_a), (_, acc_b) = lax.fori_loop(0, S // tk, body, (init, init))
    o_a = acc_a / pltpu.roll(acc_a, V_HEAD, 1)
    o_b = acc_b / pltpu.roll(acc_b, V_HEAD, 1)
    lane = lax.broadcasted_iota(jnp.int32, (tq, 2 * V_HEAD), 1)
    o_ref[...] = jnp.where(lane < V_HEAD, o_a, pltpu.roll(o_b, V_HEAD, 1)).astype(o_ref.dtype)


def _attention(q, k, v, tq, tk):
    B, S, _ = q.shape
    return pl.pallas_call(
        functools.partial(_attn_kernel, tk=tk),
        out_shape=jax.ShapeDtypeStruct((B, S, N_HEADS * V_HEAD), BF16),
        grid=(B, N_HEADS // 2, S // tq),
        in_specs=[pl.BlockSpec((None, tq, 2 * HEAD_PAD), lambda b, p, i: (b, i, p)),
                  pl.BlockSpec((None, S, 2 * HEAD_PAD), lambda b, p, i: (b, 0, p)),
                  pl.BlockSpec((None, S, 2 * HEAD_PAD), lambda b, p, i: (b, 0, p))],
        out_specs=pl.BlockSpec((None, tq, 2 * V_HEAD), lambda b, p, i: (b, i, p)),
        compiler_params=pltpu.CompilerParams(
            dimension_semantics=("arbitrary", "arbitrary", "arbitrary"),
            vmem_limit_bytes=VMEM_LIMIT),
        name="attention",
    )(q, k, v)


def _seqdft_kernel(c_ref, s_ref, f_ref, o_ref):
    y = _dot(c_ref[...], f_ref[:, :FNET_W]) + _dot(s_ref[...], f_ref[:, FNET_W:])
    o_ref[...] = y.astype(o_ref.dtype)


def _seqdft(fcs, ctab, stab, tm):
    B, S, _ = fcs.shape
    return pl.pallas_call(
        _seqdft_kernel,
        out_shape=jax.ShapeDtypeStruct((B, S, FNET_W), BF16),
        grid=(S // tm, B),
        in_specs=[pl.BlockSpec((tm, S), lambda i, b: (i, 0)),
                  pl.BlockSpec((tm, S), lambda i, b: (i, 0)),
                  pl.BlockSpec((None, S, 2 * FNET_W), lambda i, b: (b, 0, 0))],
        out_specs=pl.BlockSpec((None, tm, FNET_W), lambda i, b: (b, i, 0)),
        compiler_params=pltpu.CompilerParams(dimension_semantics=("arbitrary", "arbitrary"),
                                             vmem_limit_bytes=VMEM_LIMIT),
        name="seqdft",
    )(ctab, stab, fcs)


def _first_argmax(v, iota, n):
    m = jnp.max(v, axis=0, keepdims=True)
    i = jnp.min(jnp.where(v == m, iota, n), axis=0, keepdims=True)
    return m, i


def _route(logits_t, bias):
    tm = logits_t.shape[1]
    per = N_EXPERTS // N_EXPERT_GROUPS
    scores = jax.nn.sigmoid(logits_t)
    sel = scores + bias
    neg = jnp.float32(-jnp.inf)
    iota_p = lax.broadcasted_iota(jnp.int32, (per, tm), 0)
    iota_g = lax.broadcasted_iota(jnp.int32, (N_EXPERT_GROUPS, tm), 0)
    gscore = jnp.zeros((N_EXPERT_GROUPS, tm), F32)
    for g in range(N_EXPERT_GROUPS):
        v = sel[g * per:(g + 1) * per, :]
        m1, i1 = _first_argmax(v, iota_p, per)
        m2 = jnp.max(jnp.where(iota_p == i1, neg, v), axis=0, keepdims=True)
        gscore = jnp.where(iota_g == g, m1 + m2, gscore)
    gmask = jnp.zeros((N_EXPERT_GROUPS, tm), jnp.bool_)
    v = gscore
    for _ in range(TOPK_GROUPS):
        _, i = _first_argmax(v, iota_g, N_EXPERT_GROUPS)
        hit = iota_g == i
        gmask = jnp.logical_or(gmask, hit)
        v = jnp.where(hit, neg, v)
    gkeep = jnp.where(gmask, 1.0, 0.0)
    parts = []
    for g in range(N_EXPERT_GROUPS):
        keep = gkeep[g:g + 1, :] > 0.5
        parts.append(jnp.where(keep, sel[g * per:(g + 1) * per, :], neg))
    v = jnp.concatenate(parts, axis=0)
    iota_e = lax.broadcasted_iota(jnp.int32, (N_EXPERTS, tm), 0)
    iota_k = lax.broadcasted_iota(jnp.int32, (TOP_K, tm), 0)
    idx = jnp.zeros((TOP_K, tm), jnp.int32)
    wts = jnp.zeros((TOP_K, tm), F32)
    for kk in range(TOP_K):
        _, i = _first_argmax(v, iota_e, N_EXPERTS)
        hit = iota_e == i
        wk = jnp.sum(jnp.where(hit, scores, 0.0), axis=0, keepdims=True)
        v = jnp.where(hit, neg, v)
        idx = jnp.where(iota_k == kk, i, idx)
        wts = jnp.where(iota_k == kk, wk, wts)
    wts = wts / jnp.sum(wts, axis=0, keepdims=True) * ROUTED_SCALE
    return idx, wts


def _merge_kernel(x_ref, o_ref, bf_ref, g_ref, wa_ref, wf_ref, wo_ref, gffn_ref, wsgu_ref,
                  wsd_ref, wrt_ref, br_ref, hn_out, base_out, idx_out, w_out):
    a = _dot(o_ref[...], wa_ref[...])
    b = _dot(bf_ref[...], wf_ref[...])
    g = g_ref[...].astype(F32)
    m = (g[:, :D_MODEL] * a + g[:, D_MODEL:] * b).astype(BF16)
    x2 = x_ref[...] + _dot(m, wo_ref[...])
    hn = x2 * _rms(x2, D_MODEL) * gffn_ref[...]
    tm = hn.shape[0]
    for r in range(ROWS_PER_TOKEN):
        hn_out[pl.ds(r, tm, stride=ROWS_PER_TOKEN), :] = hn[:, r * LANES:(r + 1) * LANES]
    hb = hn.astype(BF16)
    gu = _dot(hb, wsgu_ref[...])
    act = (jax.nn.silu(gu[:, :SHARED_FF]) * gu[:, SHARED_FF:]).astype(BF16)
    base_out[...] = x2 + _dot(act, wsd_ref[...])
    logits_t = lax.dot_general(wrt_ref[...], hn, (((1,), (1,)), ((), ())),
                               precision=lax.Precision.HIGHEST, preferred_element_type=F32)
    idx, wts = _route(logits_t, br_ref[...])
    idx_out[...] = idx
    w_out[...] = wts


def _merge(x2d, o2d, bf2d, gates, wts, tm):
    T = x2d.shape[0]
    full = lambda a: pl.BlockSpec(a.shape, lambda i: (0,) * a.ndim)
    tok = lambda w: pl.BlockSpec((tm, w), lambda i: (i, 0))
    names = ['w_a', 'w_f', 'w_o', 'g_ffn', 'w_sgu', 'w_sd', 'w_rt', 'b_r']
    ws = [wts[n] for n in names]
    return pl.pallas_call(
        _merge_kernel,
        out_shape=(jax.ShapeDtypeStruct((T * ROWS_PER_TOKEN, LANES), F32),
                   jax.ShapeDtypeStruct((T, D_MODEL), F32),
                   jax.ShapeDtypeStruct((TOP_K, T), jnp.int32),
                   jax.ShapeDtypeStruct((TOP_K, T), F32)),
        grid=(T // tm,),
        in_specs=[tok(D_MODEL), tok(N_HEADS * V_HEAD), tok(FNET_W), tok(2 * D_MODEL)]
                 + [full(w) for w in ws],
        out_specs=(pl.BlockSpec((tm * ROWS_PER_TOKEN, LANES), lambda i: (i, 0)), tok(D_MODEL),
                   pl.BlockSpec((TOP_K, tm), lambda i: (0, i)),
                   pl.BlockSpec((TOP_K, tm), lambda i: (0, i))),
        compiler_params=pltpu.CompilerParams(dimension_semantics=("arbitrary",),
                                             vmem_limit_bytes=VMEM_LIMIT),
        name="merge",
    )(x2d, o2d, bf2d, gates, *ws)


def _moe_kernel(cnt_ref, start_ref, tok_hbm, w_hbm, hn_ref, wg_ref, wu_ref, wd_ref, out_ref,
                tok_s, w_s, gbuf, ybuf, sem, *, chunk):
    i = pl.program_id(0)
    e = pl.program_id(1)
    R = ROWS_PER_TOKEN

    @pl.when(jnp.logical_and(i == 0, e == 0))
    def _():
        gbuf[...] = jnp.zeros_like(gbuf)

    @pl.when(e == 0)
    def _():
        cp_t = pltpu.make_async_copy(tok_hbm.at[i], tok_s, sem.at[0])
        cp_w = pltpu.make_async_copy(w_hbm.at[i], w_s, sem.at[1])
        cp_t.start()
        cp_w.start()
        out_ref[...] = jnp.zeros_like(out_ref)
        cp_t.wait()
        cp_w.wait()

    n = cnt_ref[i * N_EXPERTS + e]
    st = start_ref[i * N_EXPERTS + e]

    def do_chunk(c, _):
        base = st + c * chunk
        rows = jnp.minimum(chunk, n - c * chunk)

        full_groups = rows // GROUP
        tail_start = full_groups * GROUP

        def tile_at(row):
            return pl.ds(pl.multiple_of(row * R, R), R)

        def gather_row(j, _):
            gbuf[tile_at(j), :] = hn_ref[tile_at(tok_s[base + j]), :]
            return 0

        def gather_group(g, _):
            j0 = g * GROUP
            vals = [hn_ref[tile_at(tok_s[base + j0 + r]), :] for r in range(GROUP)]
            for r in range(GROUP):
                gbuf[tile_at(j0 + r), :] = vals[r]
            return 0

        lax.fori_loop(0, full_groups, gather_group, 0)
        lax.fori_loop(tail_start, rows, gather_row, 0)
        xg = jnp.concatenate([gbuf[pl.ds(r, chunk, stride=R), :] for r in range(R)],
                             axis=1).astype(BF16)
        g = _dot(xg, wg_ref[0])
        u = _dot(xg, wu_ref[0])
        act = (jax.nn.silu(g) * u).astype(BF16)
        y = _dot(act, wd_ref[0])
        for r in range(R):
            ybuf[pl.ds(r, chunk, stride=R), :] = y[:, r * LANES:(r + 1) * LANES]

        def scatter_row(j, _):
            dst = tile_at(tok_s[base + j])
            out_ref[dst, :] = out_ref[dst, :] + w_s[base + j] * ybuf[tile_at(j), :]
            return 0

        def scatter_group(g, _):
            j0 = g * GROUP
            dsts = [tile_at(tok_s[base + j0 + r]) for r in range(GROUP)]
            vals = [out_ref[dsts[r], :] + w_s[base + j0 + r] * ybuf[tile_at(j0 + r), :]
                    for r in range(GROUP)]
            for r in range(GROUP):
                out_ref[dsts[r], :] = vals[r]
            return 0

        lax.fori_loop(0, full_groups, scatter_group, 0)
        lax.fori_loop(tail_start, rows, scatter_row, 0)
        return 0

    lax.fori_loop(0, pl.cdiv(n, chunk), do_chunk, 0)


def _moe(hn_tiles, cnt, start, tok_sorted, w_sorted, wts, tmoe, chunk):
    R = ROWS_PER_TOKEN
    T = hn_tiles.shape[0] // R
    n_tiles = T // tmoe
    L = tok_sorted.shape[1]
    wspec = lambda a: pl.BlockSpec((1,) + a.shape[1:], lambda i, e, *_: (e, 0, 0))
    return pl.pallas_call(
        functools.partial(_moe_kernel, chunk=chunk),
        out_shape=jax.ShapeDtypeStruct((T * R, LANES), F32),
        grid_spec=pltpu.PrefetchScalarGridSpec(
            num_scalar_prefetch=2,
            grid=(n_tiles, N_EXPERTS),
            in_specs=[pl.BlockSpec(memory_space=pl.ANY),
                      pl.BlockSpec(memory_space=pl.ANY),
                      pl.BlockSpec((tmoe * R, LANES), lambda i, e, *_: (i, 0)),
                      wspec(wts['w_eg']), wspec(wts['w_eu']), wspec(wts['w_ed'])],
            out_specs=pl.BlockSpec((tmoe * R, LANES), lambda i, e, *_: (i, 0)),
            scratch_shapes=[pltpu.SMEM((L,), jnp.int32),
                            pltpu.SMEM((L,), F32),
                            pltpu.VMEM((chunk * R, LANES), F32),
                            pltpu.VMEM((chunk * R, LANES), F32),
                            pltpu.SemaphoreType.DMA((2,))]),
        compiler_params=pltpu.CompilerParams(dimension_semantics=("arbitrary", "arbitrary"),
                                             vmem_limit_bytes=VMEM_LIMIT),
        name="moe",
    )(cnt, start, tok_sorted, w_sorted, hn_tiles, wts['w_eg'], wts['w_eu'], wts['w_ed'])


def _dispatch(idx, w, tmoe):
    T = idx.shape[1]
    n_tiles = T // tmoe
    tok = jnp.arange(T, dtype=jnp.int32)
    key = (tok // tmoe)[None, :] * N_EXPERTS + idx
    tok_l = jnp.broadcast_to((tok % tmoe)[None, :], idx.shape)
    _, tok_sorted, w_sorted = lax.sort((key.reshape(-1), tok_l.reshape(-1), w.reshape(-1)),
                                       num_keys=1)
    onehot = idx.reshape(TOP_K, n_tiles, tmoe)[..., None] == jnp.arange(N_EXPERTS, dtype=jnp.int32)
    cnt = jnp.sum(onehot.astype(jnp.int32), axis=(0, 2))
    start = jnp.cumsum(cnt, axis=1) - cnt
    tok_sorted = tok_sorted.reshape(n_tiles, TOP_K * tmoe)
    w_sorted = w_sorted.reshape(n_tiles, TOP_K * tmoe)
    return cnt.reshape(-1), start.reshape(-1), tok_sorted, w_sorted


def _ple_kernel(base_ref, routed_ref, p_ref, gple_ref, wg_ref, wp_ref, y_ref):
    tm = base_ref.shape[0]
    routed = jnp.concatenate([routed_ref[pl.ds(r, tm, stride=ROWS_PER_TOKEN), :]
                              for r in range(ROWS_PER_TOKEN)], axis=1)
    x3 = base_ref[...] + routed
    hn = (x3 * _rms(x3, D_MODEL) * gple_ref[...]).astype(BF16)
    gate = jax.nn.sigmoid(_dot(hn, wg_ref[...]))
    y_ref[...] = x3 + gate * _dot(p_ref[...].astype(BF16), wp_ref[...])


def _ple(base, routed, p2d, wts, tm):
    T = base.shape[0]
    full = lambda a: pl.BlockSpec(a.shape, lambda i: (0,) * a.ndim)
    tok = lambda w: pl.BlockSpec((tm, w), lambda i: (i, 0))
    ws = [wts['g_ple'], wts['w_pg'], wts['w_pp']]
    return pl.pallas_call(
        _ple_kernel,
        out_shape=jax.ShapeDtypeStruct((T, D_MODEL), F32),
        grid=(T // tm,),
        in_specs=[tok(D_MODEL), pl.BlockSpec((tm * ROWS_PER_TOKEN, LANES), lambda i: (i, 0)),
                  tok(PLE_DIM)] + [full(w) for w in ws],
        out_specs=tok(D_MODEL),
        compiler_params=pltpu.CompilerParams(dimension_semantics=("arbitrary",),
                                             vmem_limit_bytes=VMEM_LIMIT),
        name="ple",
    )(base, routed, p2d, *ws)


def _head_pad_cols(w, width):
    K = w.shape[0]
    w = w.reshape(K, N_HEADS, width)
    return jnp.pad(w, ((0, 0), (0, 0), (0, HEAD_PAD - width))).reshape(K, N_HEADS * HEAD_PAD)


def _rot_partner(w):
    half = QK_ROPE // 2
    z = jnp.zeros_like(w[..., :QK_NOPE])
    return jnp.concatenate([z, -w[..., QK_NOPE + half:QK_HEAD], w[..., QK_NOPE:QK_NOPE + half]], axis=-1)


def _prep_weights(g_mix, w_in, b_gate, g_qlat, w_q_up, g_kvlat, w_kv_up, g_qn, g_kn, w_a, w_f,
                  w_o, g_ffn, w_router, b_router, w_e_gate, w_e_up, w_e_down, w_s_gate, w_s_up,
                  w_s_down, g_ple, w_ple_gate, w_ple_proj):
    half = QK_ROPE // 2
    off_kv, off_kr = Q_LORA, Q_LORA + KV_LORA
    off_f = off_kr + QK_ROPE
    off_g = off_f + FNET_W
    w1 = jnp.concatenate([
        w_in[:, :off_kr],
        jnp.pad(w_in[:, off_kr:off_f], ((0, 0), (0, LANES - QK_ROPE))),
        w_in[:, off_f:]], axis=1).astype(BF16)
    wq3 = w_q_up.reshape(Q_LORA, N_HEADS, QK_HEAD)
    wq = _head_pad_cols(w_q_up, QK_HEAD).astype(BF16)
    wqr = _head_pad_cols(_rot_partner(wq3).reshape(Q_LORA, -1), QK_HEAD).astype(BF16)
    wkv3 = w_kv_up.reshape(KV_LORA, N_HEADS, QK_NOPE + V_HEAD)
    wk = _head_pad_cols(wkv3[:, :, :QK_NOPE].reshape(KV_LORA, -1), QK_NOPE).astype(BF16)
    wv = _head_pad_cols(wkv3[:, :, QK_NOPE:].reshape(KV_LORA, -1), V_HEAD).astype(BF16)
    vone = jnp.tile(jnp.concatenate([jnp.zeros((V_HEAD,), F32), jnp.ones((HEAD_PAD - V_HEAD,), F32)]),
                    N_HEADS).reshape(1, N_HEADS * HEAD_PAD)
    eye = jnp.eye(QK_ROPE, dtype=F32)
    place = jnp.concatenate([jnp.zeros((QK_ROPE, QK_NOPE), F32), eye], axis=1)
    place_rot = _rot_partner(place)
    ekr = jnp.concatenate([
        _head_pad_cols(jnp.tile(place, (1, N_HEADS)), QK_HEAD),
        _head_pad_cols(jnp.tile(place_rot, (1, N_HEADS)), QK_HEAD)], axis=1)
    ekr = jnp.pad(ekr, ((0, LANES - QK_ROPE), (0, 0))).astype(BF16)

    def gain_tabs(g):
        gp = jnp.concatenate([jnp.zeros((QK_NOPE,), F32), g[QK_NOPE + half:], g[QK_NOPE:QK_NOPE + half]])
        pad = lambda a: jnp.pad(a, (0, HEAD_PAD - QK_HEAD)).reshape(1, HEAD_PAD)
        return pad(g), pad(gp)

    gqc, gqs = gain_tabs(g_qn)
    gkc, gks = gain_tabs(g_kn)
    j = jnp.arange(FNET_GROUP_W, dtype=jnp.int32)
    ang = (2.0 * np.pi / FNET_GROUP_W) * ((j[:, None] * j[None, :]) % FNET_GROUP_W).astype(F32)
    eye_g = jnp.eye(FNET_GROUPS, dtype=F32)
    bd = jnp.concatenate([jnp.kron(eye_g, jnp.cos(ang)), jnp.kron(eye_g, jnp.sin(ang))], axis=1).astype(BF16)
    row = lambda a: a.reshape(1, -1).astype(F32)
    return {
        'g_mix': row(g_mix), 'w1': w1, 'b_gate': row(b_gate), 'g_qlat': row(g_qlat),
        'g_kvlat': row(g_kvlat), 'wq': wq, 'wqr': wqr, 'wk': wk, 'wv': wv, 'vone': vone, 'ekr': ekr,
        'gqc': gqc, 'gqs': gqs, 'gkc': gkc, 'gks': gks, 'bd': bd,
        'w_a': w_a.astype(BF16), 'w_f': w_f.astype(BF16), 'w_o': w_o.astype(BF16),
        'g_ffn': row(g_ffn),
        'w_sgu': jnp.concatenate([w_s_gate, w_s_up], axis=1).astype(BF16),
        'w_sd': w_s_down.astype(BF16),
        'w_rt': w_router.T.astype(F32), 'b_r': b_router.reshape(N_EXPERTS, 1).astype(F32),
        'w_eg': w_e_gate.astype(BF16), 'w_eu': w_e_up.astype(BF16), 'w_ed': w_e_down.astype(BF16),
        'g_ple': row(g_ple), 'w_pg': w_ple_gate.astype(BF16), 'w_pp': w_ple_proj.astype(BF16),
    }


def _tables(S):
    half = QK_ROPE // 2
    freqs = 1.0 / (ROPE_THETA ** (jnp.arange(half, dtype=F32) / half))
    ang = jnp.arange(S, dtype=F32)[:, None] * freqs[None, :]
    cos, sin = jnp.cos(ang), jnp.sin(ang)
    cos_t = jnp.concatenate([jnp.ones((S, QK_NOPE), F32), cos, cos,
                             jnp.zeros((S, HEAD_PAD - QK_HEAD), F32)], axis=1)
    sin_t = jnp.concatenate([jnp.zeros((S, QK_NOPE), F32), sin, sin,
                             jnp.zeros((S, HEAD_PAD - QK_HEAD), F32)], axis=1)
    s = jnp.arange(S, dtype=jnp.int32)
    a = (2.0 * np.pi / S) * ((s[:, None] * s[None, :]) % S).astype(F32)
    norm = (S * FNET_GROUP_W) ** -0.5
    return {'cos': cos_t, 'sin': sin_t,
            'dft_c': (jnp.cos(a) * norm).astype(BF16),
            'dft_s': (-jnp.sin(a) * norm).astype(BF16)}


def _block(x, p, wts, tabs, *, tm, tq, tk, tdft, tmoe, chunk):
    B, S, D = x.shape
    T = B * S
    x2d = x.reshape(T, D)
    q, k, v, fcs, gates = _inproj(x2d, S, tabs, wts, tm)
    o = _attention(q.reshape(B, S, -1), k.reshape(B, S, -1), v.reshape(B, S, -1), tq, tk)
    bf = _seqdft(fcs.reshape(B, S, -1), tabs['dft_c'], tabs['dft_s'], tdft)
    hn, base, idx, w = _merge(x2d, o.reshape(T, -1), bf.reshape(T, -1), gates, wts, tm)
    tmoe = min(tmoe, T)
    cnt, start, tok_sorted, w_sorted = _dispatch(idx, w, tmoe)
    routed = _moe(hn, cnt, start, tok_sorted, w_sorted, wts, tmoe, chunk)
    y = _ple(base, routed, p.reshape(T, -1), wts, tm)
    return y.reshape(B, S, D)


def kernel(x_prompt, x_sample, p_prompt, p_sample, g_mix, w_in, b_gate, g_qlat, w_q_up, g_kvlat, w_kv_up, g_qn, g_kn, w_a, w_f, w_o, g_ffn, w_router, b_router, w_e_gate, w_e_up, w_e_down, w_s_gate, w_s_up, w_s_down, g_ple, w_ple_gate, w_ple_proj):
    params = (g_mix, w_in, b_gate, g_qlat, w_q_up, g_kvlat, w_kv_up, g_qn, g_kn, w_a, w_f, w_o,
              g_ffn, w_router, b_router, w_e_gate, w_e_up, w_e_down, w_s_gate, w_s_up, w_s_down,
              g_ple, w_ple_gate, w_ple_proj)
    depth = g_mix.shape[0]
    cfg = dict(tm=512, tq=512, tk=512, tdft=512, tmoe=2048, chunk=288)
    xp, xs = x_prompt, x_sample
    tabs = _tables(x_prompt.shape[1])
    tabs_s = tabs if x_sample.shape[1] == x_prompt.shape[1] else _tables(x_sample.shape[1])
    for l in range(depth):
        wts = _prep_weights(*[a[l] for a in params])
        xp = _block(xp, p_prompt[l], wts, tabs, **cfg)
        xs = _block(xs, p_sample[l], wts, tabs_s, **cfg)
    return (xp, xs)
```

```python
import functools

import jax
import jax.numpy as jnp
import numpy as np
from jax import lax
from jax.experimental import pallas as pl
from jax.experimental.pallas import tpu as pltpu

D_MODEL = 1024
N_HEADS = 8
QK_NOPE = 64
QK_ROPE = 32
QK_HEAD = QK_NOPE + QK_ROPE
V_HEAD = 64
Q_LORA = 384
KV_LORA = 256
ROPE_THETA = 10000.0
FNET_GROUPS = 4
FNET_GROUP_W = 128
FNET_W = FNET_GROUPS * FNET_GROUP_W
N_EXPERTS = 64
TOP_K = 8
N_EXPERT_GROUPS = 8
TOPK_GROUPS = 4
EXPERT_FF = 256
SHARED_FF = 256
ROUTED_SCALE = 2.5
PLE_DIM = 256
EPS = 1e-6

LANES = 128
SUBLANES = 8
HEAD_PAD = LANES
ROWS_PER_TOKEN = D_MODEL // LANES
GROUP = 8
FFN_SPLIT = 2

C_Q = 0
C_KV = C_Q + Q_LORA
C_KR = C_KV + KV_LORA
C_F = C_KR + LANES
C_G = C_F + FNET_W
C_END = C_G + 2 * D_MODEL

VMEM_LIMIT = 56 * 1024 * 1024
BF16 = jnp.bfloat16
F32 = jnp.float32


def _rms(x, n):
    return lax.rsqrt(jnp.sum(x * x, axis=-1, keepdims=True) * (1.0 / n) + EPS)


def _dot(a, b):
    return jnp.dot(a, b, preferred_element_type=F32)


def _inproj_kernel(x_ref, cos_ref, sin_ref, gmix_ref, w1_ref, bg_ref, gq_ref, gkv_ref,
                   wq_ref, wqr_ref, wk_ref, wv_ref, vone_ref, ekr_ref, gqc_ref, gqs_ref, gkc_ref,
                   gks_ref, bd_ref, q_out, k_out, v_out, f_out, g_out):
    x = x_ref[...]
    h = (x * _rms(x, D_MODEL) * gmix_ref[...]).astype(BF16)
    proj = _dot(h, w1_ref[...])
    q_lat = proj[:, C_Q:C_KV]
    q_lat = (q_lat * _rms(q_lat, Q_LORA) * gq_ref[...]).astype(BF16)
    kv_lat = proj[:, C_KV:C_KR]
    kv_lat = (kv_lat * _rms(kv_lat, KV_LORA) * gkv_ref[...]).astype(BF16)
    kr = proj[:, C_KR:C_F].astype(BF16)
    f_in = proj[:, C_F:C_G].astype(BF16)
    g_out[...] = jax.nn.sigmoid(proj[:, C_G:C_END] + bg_ref[...]).astype(BF16)
    f_out[...] = _dot(f_in, bd_ref[...]).astype(BF16)
    v_out[...] = (_dot(kv_lat, wv_ref[...]) + vone_ref[...]).astype(BF16)

    q = _dot(q_lat, wq_ref[...])
    qr = _dot(q_lat, wqr_ref[...])
    kre = _dot(kr, ekr_ref[...])
    k = _dot(kv_lat, wk_ref[...]) + kre[:, :N_HEADS * HEAD_PAD]
    kr_rot = kre[:, N_HEADS * HEAD_PAD:]
    cos = cos_ref[...]
    sin = sin_ref[...]
    qc = gqc_ref[...] * cos
    qs = gqs_ref[...] * sin
    kc = gkc_ref[...] * cos
    ks = gks_ref[...] * sin
    scale = QK_HEAD ** -0.5 * np.log2(np.e)
    for hd in range(N_HEADS):
        sl = slice(hd * HEAD_PAD, (hd + 1) * HEAD_PAD)
        qh = q[:, sl]
        rq = _rms(qh, QK_HEAD) * scale
        q_out[:, sl] = ((qh * qc + qr[:, sl] * qs) * rq).astype(BF16)
        kh = k[:, sl]
        rk = _rms(kh, QK_HEAD)
        k_out[:, sl] = ((kh * kc + kr_rot[:, sl] * ks) * rk).astype(BF16)


def _inproj(x2d, S, tabs, wts, tm):
    T = x2d.shape[0]
    n_pos = S // tm
    full = lambda a: pl.BlockSpec(a.shape, lambda i: (0,) * a.ndim)
    tok = lambda w: pl.BlockSpec((tm, w), lambda i: (i, 0))
    pos = pl.BlockSpec((tm, HEAD_PAD), lambda i: (i % n_pos, 0))
    names = ['g_mix', 'w1', 'b_gate', 'g_qlat', 'g_kvlat', 'wq', 'wqr', 'wk', 'wv', 'vone', 'ekr',
             'gqc', 'gqs', 'gkc', 'gks', 'bd']
    ws = [wts[n] for n in names]
    return pl.pallas_call(
        _inproj_kernel,
        out_shape=(jax.ShapeDtypeStruct((T, N_HEADS * HEAD_PAD), BF16),
                   jax.ShapeDtypeStruct((T, N_HEADS * HEAD_PAD), BF16),
                   jax.ShapeDtypeStruct((T, N_HEADS * HEAD_PAD), BF16),
                   jax.ShapeDtypeStruct((T, 2 * FNET_W), BF16),
                   jax.ShapeDtypeStruct((T, 2 * D_MODEL), BF16)),
        grid=(T // tm,),
        in_specs=[tok(D_MODEL), pos, pos] + [full(w) for w in ws],
        out_specs=(tok(N_HEADS * HEAD_PAD), tok(N_HEADS * HEAD_PAD), tok(N_HEADS * HEAD_PAD),
                   tok(2 * FNET_W), tok(2 * D_MODEL)),
        compiler_params=pltpu.CompilerParams(dimension_semantics=("arbitrary",),
                                             vmem_limit_bytes=VMEM_LIMIT),
        name="inproj",
    )(x2d, tabs['cos'], tabs['sin'], *ws)


def _attn_kernel(q_ref, k_ref, v_ref, o_ref, *, tk):
    S = k_ref.shape[0]
    tq = q_ref.shape[0]
    nblk = tk // LANES

    def body(c, carry):
        off = pl.multiple_of(c * tk, tk)
        new = []
        for a in range(2):
            hs = slice(a * HEAD_PAD, (a + 1) * HEAD_PAD)
            m, acc = carry[a]
            s = lax.dot_general(q_ref[:, hs], k_ref[pl.ds(off, tk), hs],
                                (((1,), (1,)), ((), ())), preferred_element_type=F32)
            mx = s[:, :LANES]
            for j in range(1, nblk):
                mx = jnp.maximum(mx, s[:, j * LANES:(j + 1) * LANES])
            m_new = jnp.maximum(m, jnp.max(mx, axis=-1, keepdims=True))
            alpha = jnp.exp2(m - m_new)
            p = jnp.concatenate(
                [jnp.exp2((s[:, j * LANES:(j + 1) * LANES] - m_new).astype(BF16)) for j in range(nblk)],
                axis=1)
            acc = alpha * acc + _dot(p, v_ref[pl.ds(off, tk), hs])
            new.append((m_new, acc))
        return tuple(new)

    init = (jnp.full((tq, LANES), -jnp.inf, F32), jnp.zeros((tq, HEAD_PAD), F32))
    (_, acc_a), (_, acc_b) = lax.fori_loop(0, S // tk, body, (init, init))
    o_a = acc_a / pltpu.roll(acc_a, V_HEAD, 1)
    o_b = acc_b / pltpu.roll(acc_b, V_HEAD, 1)
    lane = lax.broadcasted_iota(jnp.int32, (tq, 2 * V_HEAD), 1)
    o_ref[...] = jnp.where(lane < V_HEAD, o_a, pltpu.roll(o_b, V_HEAD, 1)).astype(o_ref.dtype)


def _attention(q, k, v, tq, tk):
    B, S, _ = q.shape
    return pl.pallas_call(
        functools.partial(_attn_kernel, tk=tk),
        out_shape=jax.ShapeDtypeStruct((B, S, N_HEADS * V_HEAD), BF16),
        grid=(B, N_HEADS // 2, S // tq),
        in_specs=[pl.BlockSpec((None, tq, 2 * HEAD_PAD), lambda b, p, i: (b, i, p)),
                  pl.BlockSpec((None, S, 2 * HEAD_PAD), lambda b, p, i: (b, 0, p)),
                  pl.BlockSpec((None, S, 2 * HEAD_PAD), lambda b, p, i: (b, 0, p))],
        out_specs=pl.BlockSpec((None, tq, 2 * V_HEAD), lambda b, p, i: (b, i, p)),
        compiler_params=pltpu.CompilerParams(
            dimension_semantics=("arbitrary", "arbitrary", "arbitrary"),
            vmem_limit_bytes=VMEM_LIMIT),
        name="attention",
    )(q, k, v)


def _seqdft_kernel(c_ref, s_ref, f_ref, o_ref):
    y = _dot(c_ref[...], f_ref[:, :FNET_W]) + _dot(s_ref[...], f_ref[:, FNET_W:])
    o_ref[...] = y.astype(o_ref.dtype)


def _seqdft(fcs, ctab, stab, tm):
    B, S, _ = fcs.shape
    return pl.pallas_call(
        _seqdft_kernel,
        out_shape=jax.ShapeDtypeStruct((B, S, FNET_W), BF16),
        grid=(S // tm, B),
        in_specs=[pl.BlockSpec((tm, S), lambda i, b: (i, 0)),
                  pl.BlockSpec((tm, S), lambda i, b: (i, 0)),
                  pl.BlockSpec((None, S, 2 * FNET_W), lambda i, b: (b, 0, 0))],
        out_specs=pl.BlockSpec((None, tm, FNET_W), lambda i, b: (b, i, 0)),
        compiler_params=pltpu.CompilerParams(dimension_semantics=("arbitrary", "arbitrary"),
                                             vmem_limit_bytes=VMEM_LIMIT),
        name="seqdft",
    )(ctab, stab, fcs)


def _first_argmax(v, iota, n):
    m = jnp.max(v, axis=0, keepdims=True)
    i = jnp.min(jnp.where(v == m, iota, n), axis=0, keepdims=True)
    return m, i


def _route(logits_t, bias):
    tm = logits_t.shape[1]
    per = N_EXPERTS // N_EXPERT_GROUPS
    scores = jax.nn.sigmoid(logits_t)
    sel = scores + bias
    neg = jnp.float32(-jnp.inf)
    iota_p = lax.broadcasted_iota(jnp.int32, (per, tm), 0)
    iota_g = lax.broadcasted_iota(jnp.int32, (N_EXPERT_GROUPS, tm), 0)
    gscore = jnp.zeros((N_EXPERT_GROUPS, tm), F32)
    for g in range(N_EXPERT_GROUPS):
        v = sel[g * per:(g + 1) * per, :]
        m1, i1 = _first_argmax(v, iota_p, per)
        m2 = jnp.max(jnp.where(iota_p == i1, neg, v), axis=0, keepdims=True)
        gscore = jnp.where(iota_g == g, m1 + m2, gscore)
    gmask = jnp.zeros((N_EXPERT_GROUPS, tm), jnp.bool_)
    v = gscore
    for _ in range(TOPK_GROUPS):
        _, i = _first_argmax(v, iota_g, N_EXPERT_GROUPS)
        hit = iota_g == i
        gmask = jnp.logical_or(gmask, hit)
        v = jnp.where(hit, neg, v)
    gkeep = jnp.where(gmask, 1.0, 0.0)
    parts = []
    for g in range(N_EXPERT_GROUPS):
        keep = gkeep[g:g + 1, :] > 0.5
        parts.append(jnp.where(keep, sel[g * per:(g + 1) * per, :], neg))
    v = jnp.concatenate(parts, axis=0)
    iota_e = lax.broadcasted_iota(jnp.int32, (N_EXPERTS, tm), 0)
    iota_k = lax.broadcasted_iota(jnp.int32, (TOP_K, tm), 0)
    idx = jnp.zeros((TOP_K, tm), jnp.int32)
    wts = jnp.zeros((TOP_K, tm), F32)
    for kk in range(TOP_K):
        _, i = _first_argmax(v, iota_e, N_EXPERTS)
        hit = iota_e == i
        wk = jnp.sum(jnp.where(hit, scores, 0.0), axis=0, keepdims=True)
        v = jnp.where(hit, neg, v)
        idx = jnp.where(iota_k == kk, i, idx)
        wts = jnp.where(iota_k == kk, wk, wts)
    wts = wts / jnp.sum(wts, axis=0, keepdims=True) * ROUTED_SCALE
    return idx, wts


def _merge_kernel(x_ref, o_ref, bf_ref, g_ref, wa_ref, wf_ref, wo_ref, gffn_ref, wsgu_ref,
                  wsd_ref, wrt_ref, br_ref, hn_out, base_out, idx_out, w_out):
    a = _dot(o_ref[...], wa_ref[...])
    b = _dot(bf_ref[...], wf_ref[...])
    g = g_ref[...].astype(F32)
    m = (g[:, :D_MODEL] * a + g[:, D_MODEL:] * b).astype(BF16)
    x2 = x_ref[...] + _dot(m, wo_ref[...])
    hn = x2 * _rms(x2, D_MODEL) * gffn_ref[...]
    tm = hn.shape[0]
    for r in range(ROWS_PER_TOKEN):
        hn_out[pl.ds(r, tm, stride=ROWS_PER_TOKEN), :] = hn[:, r * LANES:(r + 1) * LANES]
    hb = hn.astype(BF16)
    gu = _dot(hb, wsgu_ref[...])
    act = (jax.nn.silu(gu[:, :SHARED_FF]) * gu[:, SHARED_FF:]).astype(BF16)
    base_out[...] = x2 + _dot(act, wsd_ref[...])
    logits_t = lax.dot_general(wrt_ref[...], hn, (((1,), (1,)), ((), ())),
                               precision=lax.Precision.HIGHEST, preferred_element_type=F32)
    idx, wts = _route(logits_t, br_ref[...])
    idx_out[...] = idx
    w_out[...] = wts


def _merge(x2d, o2d, bf2d, gates, wts, tm):
    T = x2d.shape[0]
    full = lambda a: pl.BlockSpec(a.shape, lambda i: (0,) * a.ndim)
    tok = lambda w: pl.BlockSpec((tm, w), lambda i: (i, 0))
    names = ['w_a', 'w_f', 'w_o', 'g_ffn', 'w_sgu', 'w_sd', 'w_rt', 'b_r']
    ws = [wts[n] for n in names]
    return pl.pallas_call(
        _merge_kernel,
        out_shape=(jax.ShapeDtypeStruct((T * ROWS_PER_TOKEN, LANES), F32),
                   jax.ShapeDtypeStruct((T, D_MODEL), F32),
                   jax.ShapeDtypeStruct((TOP_K, T), jnp.int32),
                   jax.ShapeDtypeStruct((TOP_K, T), F32)),
        grid=(T // tm,),
        in_specs=[tok(D_MODEL), tok(N_HEADS * V_HEAD), tok(FNET_W), tok(2 * D_MODEL)]
                 + [full(w) for w in ws],
        out_specs=(pl.BlockSpec((tm * ROWS_PER_TOKEN, LANES), lambda i: (i, 0)), tok(D_MODEL),
                   pl.BlockSpec((TOP_K, tm), lambda i: (0, i)),
                   pl.BlockSpec((TOP_K, tm), lambda i: (0, i))),
        compiler_params=pltpu.CompilerParams(dimension_semantics=("arbitrary",),
                                             vmem_limit_bytes=VMEM_LIMIT),
        name="merge",
    )(x2d, o2d, bf2d, gates, *ws)


def _moe_kernel(cnt_ref, start_ref, tok_hbm, w_hbm, hn_ref, wg_ref, wu_ref, wd_ref, out_ref,
                tok_s, w_s, gbuf, ybuf, sem, *, chunk):
    i = pl.program_id(0)
    e = pl.program_id(1)
    R = ROWS_PER_TOKEN

    @pl.when(jnp.logical_and(i == 0, e == 0))
    def _():
        gbuf[...] = jnp.zeros_like(gbuf)

    @pl.when(e == 0)
    def _():
        cp_t = pltpu.make_async_copy(tok_hbm.at[i], tok_s, sem.at[0])
        cp_w = pltpu.make_async_copy(w_hbm.at[i], w_s, sem.at[1])
        cp_t.start()
        cp_w.start()
        out_ref[...] = jnp.zeros_like(out_ref)
        cp_t.wait()
        cp_w.wait()

    n = cnt_ref[i * N_EXPERTS + e]
    st = start_ref[i * N_EXPERTS + e]

    def do_chunk(c, _):
        base = st + c * chunk
        rows = jnp.minimum(chunk, n - c * chunk)

        full_groups = rows // GROUP
        tail_start = full_groups * GROUP

        def tile_at(row):
            return pl.ds(pl.multiple_of(row * R, R), R)

        def gather_row(j, _):
            gbuf[tile_at(j), :] = hn_ref[tok_s[base + j]]
            return 0

        def gather_group(g, _):
            j0 = g * GROUP
            vals = [hn_ref[tok_s[base + j0 + r]] for r in range(GROUP)]
            for r in range(GROUP):
                gbuf[tile_at(j0 + r), :] = vals[r]
            return 0

        lax.fori_loop(0, full_groups, gather_group, 0)
        lax.fori_loop(tail_start, rows, gather_row, 0)
        blk = chunk // FFN_SPLIT
        for hb in range(FFN_SPLIT):
            xg = jnp.concatenate([gbuf[pl.ds(hb * blk * R + r, blk, stride=R), :] for r in range(R)],
                                 axis=1).astype(BF16)
            g = _dot(xg, wg_ref[0])
            u = _dot(xg, wu_ref[0])
            act = (jax.nn.silu(g) * u).astype(BF16)
            y = _dot(act, wd_ref[0])
            for r in range(R):
                ybuf[pl.ds(hb * blk * R + r, blk, stride=R), :] = y[:, r * LANES:(r + 1) * LANES]

        def scatter_row(j, _):
            t = tok_s[base + j]
            out_ref[t] = out_ref[t] + w_s[base + j] * ybuf[tile_at(j), :]
            return 0

        def scatter_group(g, _):
            j0 = g * GROUP
            toks = [tok_s[base + j0 + r] for r in range(GROUP)]
            vals = [out_ref[toks[r]] + w_s[base + j0 + r] * ybuf[tile_at(j0 + r), :]
                    for r in range(GROUP)]
            for r in range(GROUP):
                out_ref[toks[r]] = vals[r]
            return 0

        lax.fori_loop(0, full_groups, scatter_group, 0)
        lax.fori_loop(tail_start, rows, scatter_row, 0)
        return 0

    lax.fori_loop(0, pl.cdiv(n, chunk), do_chunk, 0)


def _moe(hn_tiles, cnt, start, tok_sorted, w_sorted, wts, tmoe, chunk):
    R = ROWS_PER_TOKEN
    T = hn_tiles.shape[0] // R
    n_tiles = T // tmoe
    L = tok_sorted.shape[1]
    wspec = lambda a: pl.BlockSpec((1,) + a.shape[1:], lambda i, e, *_: (e, 0, 0))
    tile_spec = pl.BlockSpec((tmoe, R, LANES), lambda i, e, *_: (i, 0, 0))
    out = pl.pallas_call(
        functools.partial(_moe_kernel, chunk=chunk),
        out_shape=jax.ShapeDtypeStruct((T, R, LANES), F32),
        grid_spec=pltpu.PrefetchScalarGridSpec(
            num_scalar_prefetch=2,
            grid=(n_tiles, N_EXPERTS),
            in_specs=[pl.BlockSpec(memory_space=pl.ANY),
                      pl.BlockSpec(memory_space=pl.ANY),
                      tile_spec,
                      wspec(wts['w_eg']), wspec(wts['w_eu']), wspec(wts['w_ed'])],
            out_specs=tile_spec,
            scratch_shapes=[pltpu.SMEM((L,), jnp.int32),
                            pltpu.SMEM((L,), F32),
                            pltpu.VMEM((chunk * R, LANES), F32),
                            pltpu.VMEM((chunk * R, LANES), F32),
                            pltpu.SemaphoreType.DMA((2,))]),
        compiler_params=pltpu.CompilerParams(dimension_semantics=("arbitrary", "arbitrary"),
                                             vmem_limit_bytes=VMEM_LIMIT),
        name="moe",
    )(cnt, start, tok_sorted, w_sorted, hn_tiles.reshape(T, R, LANES),
      wts['w_eg'], wts['w_eu'], wts['w_ed'])
    return out.reshape(T * R, LANES)


def _dispatch(idx, w, tmoe):
    T = idx.shape[1]
    n_tiles = T // tmoe
    tok = jnp.arange(T, dtype=jnp.int32)
    key = (tok // tmoe)[None, :] * N_EXPERTS + idx
    tok_l = jnp.broadcast_to((tok % tmoe)[None, :], idx.shape)
    _, tok_sorted, w_sorted = lax.sort((key.reshape(-1), tok_l.reshape(-1), w.reshape(-1)),
                                       num_keys=1)
    onehot = idx.reshape(TOP_K, n_tiles, tmoe)[..., None] == jnp.arange(N_EXPERTS, dtype=jnp.int32)
    cnt = jnp.sum(onehot.astype(jnp.int32), axis=(0, 2))
    start = jnp.cumsum(cnt, axis=1) - cnt
    tok_sorted = tok_sorted.reshape(n_tiles, TOP_K * tmoe)
    w_sorted = w_sorted.reshape(n_tiles, TOP_K * tmoe)
    return cnt.reshape(-1), start.reshape(-1), tok_sorted, w_sorted


def _ple_kernel(base_ref, routed_ref, p_ref, gple_ref, wg_ref, wp_ref, y_ref):
    tm = base_ref.shape[0]
    routed = jnp.concatenate([routed_ref[pl.ds(r, tm, stride=ROWS_PER_TOKEN), :]
                              for r in range(ROWS_PER_TOKEN)], axis=1)
    x3 = base_ref[...] + routed
    hn = (x3 * _rms(x3, D_MODEL) * gple_ref[...]).astype(BF16)
    gate = jax.nn.sigmoid(_dot(hn, wg_ref[...]))
    y_ref[...] = x3 + gate * _dot(p_ref[...].astype(BF16), wp_ref[...])


def _ple(base, routed, p2d, wts, tm):
    T = base.shape[0]
    full = lambda a: pl.BlockSpec(a.shape, lambda i: (0,) * a.ndim)
    tok = lambda w: pl.BlockSpec((tm, w), lambda i: (i, 0))
    ws = [wts['g_ple'], wts['w_pg'], wts['w_pp']]
    return pl.pallas_call(
        _ple_kernel,
        out_shape=jax.ShapeDtypeStruct((T, D_MODEL), F32),
        grid=(T // tm,),
        in_specs=[tok(D_MODEL), pl.BlockSpec((tm * ROWS_PER_TOKEN, LANES), lambda i: (i, 0)),
                  tok(PLE_DIM)] + [full(w) for w in ws],
        out_specs=tok(D_MODEL),
        compiler_params=pltpu.CompilerParams(dimension_semantics=("arbitrary",),
                                             vmem_limit_bytes=VMEM_LIMIT),
        name="ple",
    )(base, routed, p2d, *ws)


def _head_pad_cols(w, width):
    K = w.shape[0]
    w = w.reshape(K, N_HEADS, width)
    return jnp.pad(w, ((0, 0), (0, 0), (0, HEAD_PAD - width))).reshape(K, N_HEADS * HEAD_PAD)


def _rot_partner(w):
    half = QK_ROPE // 2
    z = jnp.zeros_like(w[..., :QK_NOPE])
    return jnp.concatenate([z, -w[..., QK_NOPE + half:QK_HEAD], w[..., QK_NOPE:QK_NOPE + half]], axis=-1)


def _prep_weights(g_mix, w_in, b_gate, g_qlat, w_q_up, g_kvlat, w_kv_up, g_qn, g_kn, w_a, w_f,
                  w_o, g_ffn, w_router, b_router, w_e_gate, w_e_up, w_e_down, w_s_gate, w_s_up,
                  w_s_down, g_ple, w_ple_gate, w_ple_proj):
    half = QK_ROPE // 2
    off_kv, off_kr = Q_LORA, Q_LORA + KV_LORA
    off_f = off_kr + QK_ROPE
    off_g = off_f + FNET_W
    w1 = jnp.concatenate([
        w_in[:, :off_kr],
        jnp.pad(w_in[:, off_kr:off_f], ((0, 0), (0, LANES - QK_ROPE))),
        w_in[:, off_f:]], axis=1).astype(BF16)
    wq3 = w_q_up.reshape(Q_LORA, N_HEADS, QK_HEAD)
    wq = _head_pad_cols(w_q_up, QK_HEAD).astype(BF16)
    wqr = _head_pad_cols(_rot_partner(wq3).reshape(Q_LORA, -1), QK_HEAD).astype(BF16)
    wkv3 = w_kv_up.reshape(KV_LORA, N_HEADS, QK_NOPE + V_HEAD)
    wk = _head_pad_cols(wkv3[:, :, :QK_NOPE].reshape(KV_LORA, -1), QK_NOPE).astype(BF16)
    wv = _head_pad_cols(wkv3[:, :, QK_NOPE:].reshape(KV_LORA, -1), V_HEAD).astype(BF16)
    vone = jnp.tile(jnp.concatenate([jnp.zeros((V_HEAD,), F32), jnp.ones((HEAD_PAD - V_HEAD,), F32)]),
                    N_HEADS).reshape(1, N_HEADS * HEAD_PAD)
    eye = jnp.eye(QK_ROPE, dtype=F32)
    place = jnp.concatenate([jnp.zeros((QK_ROPE, QK_NOPE), F32), eye], axis=1)
    place_rot = _rot_partner(place)
    ekr = jnp.concatenate([
        _head_pad_cols(jnp.tile(place, (1, N_HEADS)), QK_HEAD),
        _head_pad_cols(jnp.tile(place_rot, (1, N_HEADS)), QK_HEAD)], axis=1)
    ekr = jnp.pad(ekr, ((0, LANES - QK_ROPE), (0, 0))).astype(BF16)

    def gain_tabs(g):
        gp = jnp.concatenate([jnp.zeros((QK_NOPE,), F32), g[QK_NOPE + half:], g[QK_NOPE:QK_NOPE + half]])
        pad = lambda a: jnp.pad(a, (0, HEAD_PAD - QK_HEAD)).reshape(1, HEAD_PAD)
        return pad(g), pad(gp)

    gqc, gqs = gain_tabs(g_qn)
    gkc, gks = gain_tabs(g_kn)
    j = jnp.arange(FNET_GROUP_W, dtype=jnp.int32)
    ang = (2.0 * np.pi / FNET_GROUP_W) * ((j[:, None] * j[None, :]) % FNET_GROUP_W).astype(F32)
    eye_g = jnp.eye(FNET_GROUPS, dtype=F32)
    bd = jnp.concatenate([jnp.kron(eye_g, jnp.cos(ang)), jnp.kron(eye_g, jnp.sin(ang))], axis=1).astype(BF16)
    row = lambda a: a.reshape(1, -1).astype(F32)
    return {
        'g_mix': row(g_mix), 'w1': w1, 'b_gate': row(b_gate), 'g_qlat': row(g_qlat),
        'g_kvlat': row(g_kvlat), 'wq': wq, 'wqr': wqr, 'wk': wk, 'wv': wv, 'vone': vone, 'ekr': ekr,
        'gqc': gqc, 'gqs': gqs, 'gkc': gkc, 'gks': gks, 'bd': bd,
        'w_a': w_a.astype(BF16), 'w_f': w_f.astype(BF16), 'w_o': w_o.astype(BF16),
        'g_ffn': row(g_ffn),
        'w_sgu': jnp.concatenate([w_s_gate, w_s_up], axis=1).astype(BF16),
        'w_sd': w_s_down.astype(BF16),
        'w_rt': w_router.T.astype(F32), 'b_r': b_router.reshape(N_EXPERTS, 1).astype(F32),
        'w_eg': w_e_gate.astype(BF16), 'w_eu': w_e_up.astype(BF16), 'w_ed': w_e_down.astype(BF16),
        'g_ple': row(g_ple), 'w_pg': w_ple_gate.astype(BF16), 'w_pp': w_ple_proj.astype(BF16),
    }


def _tables(S):
    half = QK_ROPE // 2
    freqs = 1.0 / (ROPE_THETA ** (jnp.arange(half, dtype=F32) / half))
    ang = jnp.arange(S, dtype=F32)[:, None] * freqs[None, :]
    cos, sin = jnp.cos(ang), jnp.sin(ang)
    cos_t = jnp.concatenate([jnp.ones((S, QK_NOPE), F32), cos, cos,
                             jnp.zeros((S, HEAD_PAD - QK_HEAD), F32)], axis=1)
    sin_t = jnp.concatenate([jnp.zeros((S, QK_NOPE), F32), sin, sin,
                             jnp.zeros((S, HEAD_PAD - QK_HEAD), F32)], axis=1)
    s = jnp.arange(S, dtype=jnp.int32)
    a = (2.0 * np.pi / S) * ((s[:, None] * s[None, :]) % S).astype(F32)
    norm = (S * FNET_GROUP_W) ** -0.5
    return {'cos': cos_t, 'sin': sin_t,
            'dft_c': (jnp.cos(a) * norm).astype(BF16),
            'dft_s': (-jnp.sin(a) * norm).astype(BF16)}


def _block(x, p, wts, tabs, *, tm, tq, tk, tdft, tmoe, chunk):
    B, S, D = x.shape
    T = B * S
    x2d = x.reshape(T, D)
    q, k, v, fcs, gates = _inproj(x2d, S, tabs, wts, tm)
    o = _attention(q.reshape(B, S, -1), k.reshape(B, S, -1), v.reshape(B, S, -1), tq, tk)
    bf = _seqdft(fcs.reshape(B, S, -1), tabs['dft_c'], tabs['dft_s'], tdft)
    hn, base, idx, w = _merge(x2d, o.reshape(T, -1), bf.reshape(T, -1), gates, wts, tm)
    tmoe = min(tmoe, T)
    cnt, start, tok_sorted, w_sorted = _dispatch(idx, w, tmoe)
    routed = _moe(hn, cnt, start, tok_sorted, w_sorted, wts, tmoe, chunk)
    y = _ple(base, routed, p.reshape(T, -1), wts, tm)
    return y.reshape(B, S, D)


def kernel(x_prompt, x_sample, p_prompt, p_sample, g_mix, w_in, b_gate, g_qlat, w_q_up, g_kvlat, w_kv_up, g_qn, g_kn, w_a, w_f, w_o, g_ffn, w_router, b_router, w_e_gate, w_e_up, w_e_down, w_s_gate, w_s_up, w_s_down, g_ple, w_ple_gate, w_ple_proj):
    params = (g_mix, w_in, b_gate, g_qlat, w_q_up, g_kvlat, w_kv_up, g_qn, g_kn, w_a, w_f, w_o,
              g_ffn, w_router, b_router, w_e_gate, w_e_up, w_e_down, w_s_gate, w_s_up, w_s_down,
              g_ple, w_ple_gate, w_ple_proj)
    depth = g_mix.shape[0]
    cfg = dict(tm=512, tq=512, tk=1024, tdft=512, tmoe=2048, chunk=288)
    xp, xs = x_prompt, x_sample
    tabs = _tables(x_prompt.shape[1])
    tabs_s = tabs if x_sample.shape[1] == x_prompt.shape[1] else _tables(x_sample.shape[1])
    for l in range(depth):
        wts = _prep_weights(*[a[l] for a in params])
        xp = _block(xp, p_prompt[l], wts, tabs, **cfg)
        xs = _block(xs, p_sample[l], wts, tabs_s, **cfg)
    return (xp, xs)
```

```python
import functools

import jax
import jax.numpy as jnp
import numpy as np
from jax import lax
from jax.experimental import pallas as pl
from jax.experimental.pallas import tpu as pltpu

D_MODEL = 1024
N_HEADS = 8
QK_NOPE = 64
QK_ROPE = 32
QK_HEAD = QK_NOPE + QK_ROPE
V_HEAD = 64
Q_LORA = 384
KV_LORA = 256
ROPE_THETA = 10000.0
FNET_GROUPS = 4
FNET_GROUP_W = 128
FNET_W = FNET_GROUPS * FNET_GROUP_W
N_EXPERTS = 64
TOP_K = 8
N_EXPERT_GROUPS = 8
TOPK_GROUPS = 4
EXPERT_FF = 256
SHARED_FF = 256
ROUTED_SCALE = 2.5
PLE_DIM = 256
EPS = 1e-6

LANES = 128
SUBLANES = 8
HEAD_PAD = LANES
ROWS_PER_TOKEN = D_MODEL // LANES
GROUP = 8
HEADS_PER_STEP = 2

C_Q = 0
C_KV = C_Q + Q_LORA
C_KR = C_KV + KV_LORA
C_F = C_KR + LANES
C_G = C_F + FNET_W
C_END = C_G + 2 * D_MODEL

VMEM_LIMIT = 56 * 1024 * 1024
BF16 = jnp.bfloat16
F32 = jnp.float32


def _rms(x, n):
    return lax.rsqrt(jnp.sum(x * x, axis=-1, keepdims=True) * (1.0 / n) + EPS)


def _dot(a, b):
    return jnp.dot(a, b, preferred_element_type=F32)


def _inproj_kernel(x_ref, cos_ref, sin_ref, gmix_ref, w1_ref, bg_ref, gq_ref, gkv_ref,
                   wq_ref, wqr_ref, wk_ref, wv_ref, vone_ref, ekr_ref, gqc_ref, gqs_ref, gkc_ref,
                   gks_ref, bd_ref, q_out, k_out, v_out, f_out, g_out):
    x = x_ref[...]
    h = (x * _rms(x, D_MODEL) * gmix_ref[...]).astype(BF16)
    proj = _dot(h, w1_ref[...])
    q_lat = proj[:, C_Q:C_KV]
    q_lat = (q_lat * _rms(q_lat, Q_LORA) * gq_ref[...]).astype(BF16)
    kv_lat = proj[:, C_KV:C_KR]
    kv_lat = (kv_lat * _rms(kv_lat, KV_LORA) * gkv_ref[...]).astype(BF16)
    kr = proj[:, C_KR:C_F].astype(BF16)
    f_in = proj[:, C_F:C_G].astype(BF16)
    g_out[...] = jax.nn.sigmoid(proj[:, C_G:C_END] + bg_ref[...]).astype(BF16)
    f_out[...] = _dot(f_in, bd_ref[...]).astype(BF16)
    v_out[...] = (_dot(kv_lat, wv_ref[...]) + vone_ref[...]).astype(BF16)

    q = _dot(q_lat, wq_ref[...])
    qr = _dot(q_lat, wqr_ref[...])
    kre = _dot(kr, ekr_ref[...])
    k = _dot(kv_lat, wk_ref[...]) + kre[:, :N_HEADS * HEAD_PAD]
    kr_rot = kre[:, N_HEADS * HEAD_PAD:]
    cos = cos_ref[...]
    sin = sin_ref[...]
    qc = gqc_ref[...] * cos
    qs = gqs_ref[...] * sin
    kc = gkc_ref[...] * cos
    ks = gks_ref[...] * sin
    scale = QK_HEAD ** -0.5 * np.log2(np.e)
    for hd in range(N_HEADS):
        sl = slice(hd * HEAD_PAD, (hd + 1) * HEAD_PAD)
        qh = q[:, sl]
        rq = _rms(qh, QK_HEAD) * scale
        q_out[:, sl] = ((qh * qc + qr[:, sl] * qs) * rq).astype(BF16)
        kh = k[:, sl]
        rk = _rms(kh, QK_HEAD)
        k_out[:, sl] = ((kh * kc + kr_rot[:, sl] * ks) * rk).astype(BF16)


def _inproj(x2d, S, tabs, wts, tm):
    T = x2d.shape[0]
    n_pos = S // tm
    full = lambda a: pl.BlockSpec(a.shape, lambda i: (0,) * a.ndim)
    tok = lambda w: pl.BlockSpec((tm, w), lambda i: (i, 0))
    pos = pl.BlockSpec((tm, HEAD_PAD), lambda i: (i % n_pos, 0))
    names = ['g_mix', 'w1', 'b_gate', 'g_qlat', 'g_kvlat', 'wq', 'wqr', 'wk', 'wv', 'vone', 'ekr',
             'gqc', 'gqs', 'gkc', 'gks', 'bd']
    ws = [wts[n] for n in names]
    return pl.pallas_call(
        _inproj_kernel,
        out_shape=(jax.ShapeDtypeStruct((T, N_HEADS * HEAD_PAD), BF16),
                   jax.ShapeDtypeStruct((T, N_HEADS * HEAD_PAD), BF16),
                   jax.ShapeDtypeStruct((T, N_HEADS * HEAD_PAD), BF16),
                   jax.ShapeDtypeStruct((T, 2 * FNET_W), BF16),
                   jax.ShapeDtypeStruct((T, 2 * D_MODEL), BF16)),
        grid=(T // tm,),
        in_specs=[tok(D_MODEL), pos, pos] + [full(w) for w in ws],
        out_specs=(tok(N_HEADS * HEAD_PAD), tok(N_HEADS * HEAD_PAD), tok(N_HEADS * HEAD_PAD),
                   tok(2 * FNET_W), tok(2 * D_MODEL)),
        compiler_params=pltpu.CompilerParams(dimension_semantics=("arbitrary",),
                                             vmem_limit_bytes=VMEM_LIMIT),
        name="inproj",
    )(x2d, tabs['cos'], tabs['sin'], *ws)


def _attn_kernel(q_ref, k_ref, v_ref, o_ref, s_a, s_b, *, tk):
    S = k_ref.shape[0]
    tq = q_ref.shape[0]
    nblk = tk // LANES
    n_chunks = S // tk
    heads = [slice(a * HEAD_PAD, (a + 1) * HEAD_PAD) for a in range(HEADS_PER_STEP)]
    s_bufs = (s_a, s_b)

    def scores(c, s_ref):
        for a, hs in enumerate(heads):
            s_ref[a] = lax.dot_general(q_ref[:, hs], k_ref[c * tk:(c + 1) * tk, hs],
                                       (((1,), (1,)), ((), ())), preferred_element_type=F32)

    def consume(c, s_ref, carry):
        new = []
        for a, hs in enumerate(heads):
            m, acc = carry[a]
            mx = s_ref[a, :, :LANES]
            for j in range(1, nblk):
                mx = jnp.maximum(mx, s_ref[a, :, j * LANES:(j + 1) * LANES])
            m_new = jnp.maximum(m, jnp.max(mx, axis=-1, keepdims=True))
            alpha = jnp.exp2(m - m_new)
            p = jnp.concatenate(
                [jnp.exp2((s_ref[a, :, j * LANES:(j + 1) * LANES] - m_new).astype(BF16))
                 for j in range(nblk)], axis=1)
            acc = alpha * acc + _dot(p, v_ref[c * tk:(c + 1) * tk, hs])
            new.append((m_new, acc))
        return new

    carry = [(jnp.full((tq, LANES), -jnp.inf, F32), jnp.zeros((tq, HEAD_PAD), F32))
             for _ in heads]
    scores(0, s_bufs[0])
    for c in range(n_chunks):
        if c + 1 < n_chunks:
            scores(c + 1, s_bufs[(c + 1) % 2])
        carry = consume(c, s_bufs[c % 2], carry)
    lane = lax.broadcasted_iota(jnp.int32, (tq, 2 * V_HEAD), 1)
    for pair in range(HEADS_PER_STEP // 2):
        acc_a, acc_b = carry[2 * pair][1], carry[2 * pair + 1][1]
        o_a = acc_a / pltpu.roll(acc_a, V_HEAD, 1)
        o_b = acc_b / pltpu.roll(acc_b, V_HEAD, 1)
        o_ref[:, pair * 2 * V_HEAD:(pair + 1) * 2 * V_HEAD] = jnp.where(
            lane < V_HEAD, o_a, pltpu.roll(o_b, V_HEAD, 1)).astype(o_ref.dtype)


def _attention(q, k, v, tq, tk):
    B, S, _ = q.shape
    hp = HEADS_PER_STEP
    return pl.pallas_call(
        functools.partial(_attn_kernel, tk=tk),
        out_shape=jax.ShapeDtypeStruct((B, S, N_HEADS * V_HEAD), BF16),
        grid=(B, N_HEADS // hp, S // tq),
        in_specs=[pl.BlockSpec((None, tq, hp * HEAD_PAD), lambda b, p, i: (b, i, p)),
                  pl.BlockSpec((None, S, hp * HEAD_PAD), lambda b, p, i: (b, 0, p)),
                  pl.BlockSpec((None, S, hp * HEAD_PAD), lambda b, p, i: (b, 0, p))],
        out_specs=pl.BlockSpec((None, tq, hp * V_HEAD), lambda b, p, i: (b, i, p)),
        scratch_shapes=[pltpu.VMEM((hp, tq, tk), F32), pltpu.VMEM((hp, tq, tk), F32)],
        compiler_params=pltpu.CompilerParams(
            dimension_semantics=("arbitrary", "arbitrary", "arbitrary"),
            vmem_limit_bytes=VMEM_LIMIT),
        name="attention",
    )(q, k, v)


def _seqdft_kernel(c_ref, s_ref, f_ref, o_ref):
    y = _dot(c_ref[...], f_ref[:, :FNET_W]) + _dot(s_ref[...], f_ref[:, FNET_W:])
    o_ref[...] = y.astype(o_ref.dtype)


def _seqdft(fcs, ctab, stab, tm):
    B, S, _ = fcs.shape
    return pl.pallas_call(
        _seqdft_kernel,
        out_shape=jax.ShapeDtypeStruct((B, S, FNET_W), BF16),
        grid=(S // tm, B),
        in_specs=[pl.BlockSpec((tm, S), lambda i, b: (i, 0)),
                  pl.BlockSpec((tm, S), lambda i, b: (i, 0)),
                  pl.BlockSpec((None, S, 2 * FNET_W), lambda i, b: (b, 0, 0))],
        out_specs=pl.BlockSpec((None, tm, FNET_W), lambda i, b: (b, i, 0)),
        compiler_params=pltpu.CompilerParams(dimension_semantics=("arbitrary", "arbitrary"),
                                             vmem_limit_bytes=VMEM_LIMIT),
        name="seqdft",
    )(ctab, stab, fcs)


def _first_argmax(v, iota, n):
    m = jnp.max(v, axis=0, keepdims=True)
    i = jnp.min(jnp.where(v == m, iota, n), axis=0, keepdims=True)
    return m, i


def _route(logits_t, bias):
    tm = logits_t.shape[1]
    per = N_EXPERTS // N_EXPERT_GROUPS
    scores = jax.nn.sigmoid(logits_t)
    sel = scores + bias
    neg = jnp.float32(-jnp.inf)
    iota_p = lax.broadcasted_iota(jnp.int32, (per, tm), 0)
    iota_g = lax.broadcasted_iota(jnp.int32, (N_EXPERT_GROUPS, tm), 0)
    gscore = jnp.zeros((N_EXPERT_GROUPS, tm), F32)
    for g in range(N_EXPERT_GROUPS):
        v = sel[g * per:(g + 1) * per, :]
        m1, i1 = _first_argmax(v, iota_p, per)
        m2 = jnp.max(jnp.where(iota_p == i1, neg, v), axis=0, keepdims=True)
        gscore = jnp.where(iota_g == g, m1 + m2, gscore)
    gmask = jnp.zeros((N_EXPERT_GROUPS, tm), jnp.bool_)
    v = gscore
    for _ in range(TOPK_GROUPS):
        _, i = _first_argmax(v, iota_g, N_EXPERT_GROUPS)
        hit = iota_g == i
        gmask = jnp.logical_or(gmask, hit)
        v = jnp.where(hit, neg, v)
    gkeep = jnp.where(gmask, 1.0, 0.0)
    parts = []
    for g in range(N_EXPERT_GROUPS):
        keep = gkeep[g:g + 1, :] > 0.5
        parts.append(jnp.where(keep, sel[g * per:(g + 1) * per, :], neg))
    v = jnp.concatenate(parts, axis=0)
    iota_e = lax.broadcasted_iota(jnp.int32, (N_EXPERTS, tm), 0)
    iota_k = lax.broadcasted_iota(jnp.int32, (TOP_K, tm), 0)
    idx = jnp.zeros((TOP_K, tm), jnp.int32)
    wts = jnp.zeros((TOP_K, tm), F32)
    for kk in range(TOP_K):
        _, i = _first_argmax(v, iota_e, N_EXPERTS)
        hit = iota_e == i
        wk = jnp.sum(jnp.where(hit, scores, 0.0), axis=0, keepdims=True)
        v = jnp.where(hit, neg, v)
        idx = jnp.where(iota_k == kk, i, idx)
        wts = jnp.where(iota_k == kk, wk, wts)
    wts = wts / jnp.sum(wts, axis=0, keepdims=True) * ROUTED_SCALE
    return idx, wts


def _merge_kernel(x_ref, o_ref, bf_ref, g_ref, wa_ref, wf_ref, wo_ref, gffn_ref, wsgu_ref,
                  wsd_ref, wrt_ref, br_ref, hn_out, base_out, idx_out, w_out):
    a = _dot(o_ref[...], wa_ref[...])
    b = _dot(bf_ref[...], wf_ref[...])
    g = g_ref[...].astype(F32)
    m = (g[:, :D_MODEL] * a + g[:, D_MODEL:] * b).astype(BF16)
    x2 = x_ref[...] + _dot(m, wo_ref[...])
    hn = x2 * _rms(x2, D_MODEL) * gffn_ref[...]
    tm = hn.shape[0]
    for r in range(ROWS_PER_TOKEN):
        hn_out[pl.ds(r, tm, stride=ROWS_PER_TOKEN), :] = hn[:, r * LANES:(r + 1) * LANES]
    hb = hn.astype(BF16)
    gu = _dot(hb, wsgu_ref[...])
    act = (jax.nn.silu(gu[:, :SHARED_FF]) * gu[:, SHARED_FF:]).astype(BF16)
    base_out[...] = x2 + _dot(act, wsd_ref[...])
    logits_t = lax.dot_general(wrt_ref[...], hn, (((1,), (1,)), ((), ())),
                               precision=lax.Precision.HIGHEST, preferred_element_type=F32)
    idx, wts = _route(logits_t, br_ref[...])
    idx_out[...] = idx
    w_out[...] = wts


def _merge(x2d, o2d, bf2d, gates, wts, tm):
    T = x2d.shape[0]
    full = lambda a: pl.BlockSpec(a.shape, lambda i: (0,) * a.ndim)
    tok = lambda w: pl.BlockSpec((tm, w), lambda i: (i, 0))
    names = ['w_a', 'w_f', 'w_o', 'g_ffn', 'w_sgu', 'w_sd', 'w_rt', 'b_r']
    ws = [wts[n] for n in names]
    return pl.pallas_call(
        _merge_kernel,
        out_shape=(jax.ShapeDtypeStruct((T * ROWS_PER_TOKEN, LANES), F32),
                   jax.ShapeDtypeStruct((T, D_MODEL), F32),
                   jax.ShapeDtypeStruct((TOP_K, T), jnp.int32),
                   jax.ShapeDtypeStruct((TOP_K, T), F32)),
        grid=(T // tm,),
        in_specs=[tok(D_MODEL), tok(N_HEADS * V_HEAD), tok(FNET_W), tok(2 * D_MODEL)]
                 + [full(w) for w in ws],
        out_specs=(pl.BlockSpec((tm * ROWS_PER_TOKEN, LANES), lambda i: (i, 0)), tok(D_MODEL),
                   pl.BlockSpec((TOP_K, tm), lambda i: (0, i)),
                   pl.BlockSpec((TOP_K, tm), lambda i: (0, i))),
        compiler_params=pltpu.CompilerParams(dimension_semantics=("arbitrary",),
                                             vmem_limit_bytes=VMEM_LIMIT),
        name="merge",
    )(x2d, o2d, bf2d, gates, *ws)


def _moe_kernel(cnt_ref, start_ref, tok_hbm, w_hbm, hn_ref, wg_ref, wu_ref, wd_ref, out_ref,
                tok_s, w_s, gbuf, ybuf, sem, *, chunk):
    i = pl.program_id(0)
    e = pl.program_id(1)
    R = ROWS_PER_TOKEN

    @pl.when(jnp.logical_and(i == 0, e == 0))
    def _():
        gbuf[...] = jnp.zeros_like(gbuf)

    @pl.when(e == 0)
    def _():
        cp_t = pltpu.make_async_copy(tok_hbm.at[i], tok_s, sem.at[0])
        cp_w = pltpu.make_async_copy(w_hbm.at[i], w_s, sem.at[1])
        cp_t.start()
        cp_w.start()
        out_ref[...] = jnp.zeros_like(out_ref)
        cp_t.wait()
        cp_w.wait()

    n = cnt_ref[i * N_EXPERTS + e]
    st = start_ref[i * N_EXPERTS + e]

    def do_chunk(c, _):
        base = st + c * chunk
        rows = jnp.minimum(chunk, n - c * chunk)

        full_groups = rows // GROUP
        tail_start = full_groups * GROUP

        def tile_at(row):
            return pl.ds(pl.multiple_of(row * R, R), R)

        def gather_row(j, _):
            gbuf[tile_at(j), :] = hn_ref[tok_s[base + j]]
            return 0

        def gather_group(g, _):
            j0 = g * GROUP
            vals = [hn_ref[tok_s[base + j0 + r]] for r in range(GROUP)]
            for r in range(GROUP):
                gbuf[tile_at(j0 + r), :] = vals[r]
            return 0

        lax.fori_loop(0, full_groups, gather_group, 0)
        lax.fori_loop(tail_start, rows, gather_row, 0)
        xg = jnp.concatenate([gbuf[pl.ds(r, chunk, stride=R), :] for r in range(R)],
                             axis=1).astype(BF16)
        g = _dot(xg, wg_ref[0])
        u = _dot(xg, wu_ref[0])
        act = (jax.nn.silu(g) * u).astype(BF16)
        y = _dot(act, wd_ref[0])
        for r in range(R):
            ybuf[pl.ds(r, chunk, stride=R), :] = y[:, r * LANES:(r + 1) * LANES]

        def scatter_row(j, _):
            t = tok_s[base + j]
            out_ref[t] = out_ref[t] + w_s[base + j] * ybuf[tile_at(j), :]
            return 0

        def scatter_group(g, _):
            j0 = g * GROUP
            toks = [tok_s[base + j0 + r] for r in range(GROUP)]
            vals = [out_ref[toks[r]] + w_s[base + j0 + r] * ybuf[tile_at(j0 + r), :]
                    for r in range(GROUP)]
            for r in range(GROUP):
                out_ref[toks[r]] = vals[r]
            return 0

        lax.fori_loop(0, full_groups, scatter_group, 0)
        lax.fori_loop(tail_start, rows, scatter_row, 0)
        return 0

    lax.fori_loop(0, pl.cdiv(n, chunk), do_chunk, 0)


def _moe(hn_tiles, cnt, start, tok_sorted, w_sorted, wts, tmoe, chunk):
    R = ROWS_PER_TOKEN
    T = hn_tiles.shape[0] // R
    n_tiles = T // tmoe
    L = tok_sorted.shape[1]
    wspec = lambda a: pl.BlockSpec((1,) + a.shape[1:], lambda i, e, *_: (e, 0, 0))
    tile_spec = pl.BlockSpec((tmoe, R, LANES), lambda i, e, *_: (i, 0, 0))
    out = pl.pallas_call(
        functools.partial(_moe_kernel, chunk=chunk),
        out_shape=jax.ShapeDtypeStruct((T, R, LANES), F32),
        grid_spec=pltpu.PrefetchScalarGridSpec(
            num_scalar_prefetch=2,
            grid=(n_tiles, N_EXPERTS),
            in_specs=[pl.BlockSpec(memory_space=pl.ANY),
                      pl.BlockSpec(memory_space=pl.ANY),
                      tile_spec,
                      wspec(wts['w_eg']), wspec(wts['w_eu']), wspec(wts['w_ed'])],
            out_specs=tile_spec,
            scratch_shapes=[pltpu.SMEM((L,), jnp.int32),
                            pltpu.SMEM((L,), F32),
                            pltpu.VMEM((chunk * R, LANES), F32),
                            pltpu.VMEM((chunk * R, LANES), F32),
                            pltpu.SemaphoreType.DMA((2,))]),
        compiler_params=pltpu.CompilerParams(dimension_semantics=("arbitrary", "arbitrary"),
                                             vmem_limit_bytes=VMEM_LIMIT),
        name="moe",
    )(cnt, start, tok_sorted, w_sorted, hn_tiles.reshape(T, R, LANES),
      wts['w_eg'], wts['w_eu'], wts['w_ed'])
    return out.reshape(T * R, LANES)


def _dispatch(idx, w, tmoe):
    T = idx.shape[1]
    n_tiles = T // tmoe
    per_tile = lambda a: a.reshape(TOP_K, n_tiles, tmoe).transpose(1, 0, 2).reshape(n_tiles, TOP_K * tmoe)
    e_t, w_t = per_tile(idx), per_tile(w)
    tok_l = jnp.broadcast_to(jnp.tile(jnp.arange(tmoe, dtype=jnp.int32), TOP_K)[None, :], e_t.shape)
    _, tok_sorted, w_sorted = lax.sort((e_t, tok_l, w_t), dimension=1, num_keys=1)
    onehot = e_t[..., None] == jnp.arange(N_EXPERTS, dtype=jnp.int32)
    cnt = jnp.sum(onehot.astype(jnp.int32), axis=1)
    start = jnp.cumsum(cnt, axis=1) - cnt
    return cnt.reshape(-1), start.reshape(-1), tok_sorted, w_sorted


def _ple_kernel(base_ref, routed_ref, p_ref, gple_ref, wg_ref, wp_ref, y_ref):
    tm = base_ref.shape[0]
    routed = jnp.concatenate([routed_ref[pl.ds(r, tm, stride=ROWS_PER_TOKEN), :]
                              for r in range(ROWS_PER_TOKEN)], axis=1)
    x3 = base_ref[...] + routed
    hn = (x3 * _rms(x3, D_MODEL) * gple_ref[...]).astype(BF16)
    gate = jax.nn.sigmoid(_dot(hn, wg_ref[...]))
    y_ref[...] = x3 + gate * _dot(p_ref[...].astype(BF16), wp_ref[...])


def _ple(base, routed, p2d, wts, tm):
    T = base.shape[0]
    full = lambda a: pl.BlockSpec(a.shape, lambda i: (0,) * a.ndim)
    tok = lambda w: pl.BlockSpec((tm, w), lambda i: (i, 0))
    ws = [wts['g_ple'], wts['w_pg'], wts['w_pp']]
    return pl.pallas_call(
        _ple_kernel,
        out_shape=jax.ShapeDtypeStruct((T, D_MODEL), F32),
        grid=(T // tm,),
        in_specs=[tok(D_MODEL), pl.BlockSpec((tm * ROWS_PER_TOKEN, LANES), lambda i: (i, 0)),
                  tok(PLE_DIM)] + [full(w) for w in ws],
        out_specs=tok(D_MODEL),
        compiler_params=pltpu.CompilerParams(dimension_semantics=("arbitrary",),
                                             vmem_limit_bytes=VMEM_LIMIT),
        name="ple",
    )(base, routed, p2d, *ws)


def _head_pad_cols(w, width):
    K = w.shape[0]
    w = w.reshape(K, N_HEADS, width)
    return jnp.pad(w, ((0, 0), (0, 0), (0, HEAD_PAD - width))).reshape(K, N_HEADS * HEAD_PAD)


def _rot_partner(w):
    half = QK_ROPE // 2
    z = jnp.zeros_like(w[..., :QK_NOPE])
    return jnp.concatenate([z, -w[..., QK_NOPE + half:QK_HEAD], w[..., QK_NOPE:QK_NOPE + half]], axis=-1)


def _prep_weights(g_mix, w_in, b_gate, g_qlat, w_q_up, g_kvlat, w_kv_up, g_qn, g_kn, w_a, w_f,
                  w_o, g_ffn, w_router, b_router, w_e_gate, w_e_up, w_e_down, w_s_gate, w_s_up,
                  w_s_down, g_ple, w_ple_gate, w_ple_proj):
    half = QK_ROPE // 2
    off_kv, off_kr = Q_LORA, Q_LORA + KV_LORA
    off_f = off_kr + QK_ROPE
    off_g = off_f + FNET_W
    w1 = jnp.concatenate([
        w_in[:, :off_kr],
        jnp.pad(w_in[:, off_kr:off_f], ((0, 0), (0, LANES - QK_ROPE))),
        w_in[:, off_f:]], axis=1).astype(BF16)
    wq3 = w_q_up.reshape(Q_LORA, N_HEADS, QK_HEAD)
    wq = _head_pad_cols(w_q_up, QK_HEAD).astype(BF16)
    wqr = _head_pad_cols(_rot_partner(wq3).reshape(Q_LORA, -1), QK_HEAD).astype(BF16)
    wkv3 = w_kv_up.reshape(KV_LORA, N_HEADS, QK_NOPE + V_HEAD)
    wk = _head_pad_cols(wkv3[:, :, :QK_NOPE].reshape(KV_LORA, -1), QK_NOPE).astype(BF16)
    wv = _head_pad_cols(wkv3[:, :, QK_NOPE:].reshape(KV_LORA, -1), V_HEAD).astype(BF16)
    vone = jnp.tile(jnp.concatenate([jnp.zeros((V_HEAD,), F32), jnp.ones((HEAD_PAD - V_HEAD,), F32)]),
                    N_HEADS).reshape(1, N_HEADS * HEAD_PAD)
    eye = jnp.eye(QK_ROPE, dtype=F32)
    place = jnp.concatenate([jnp.zeros((QK_ROPE, QK_NOPE), F32), eye], axis=1)
    place_rot = _rot_partner(place)
    ekr = jnp.concatenate([
        _head_pad_cols(jnp.tile(place, (1, N_HEADS)), QK_HEAD),
        _head_pad_cols(jnp.tile(place_rot, (1, N_HEADS)), QK_HEAD)], axis=1)
    ekr = jnp.pad(ekr, ((0, LANES - QK_ROPE), (0, 0))).astype(BF16)

    def gain_tabs(g):
        gp = jnp.concatenate([jnp.zeros((QK_NOPE,), F32), g[QK_NOPE + half:], g[QK_NOPE:QK_NOPE + half]])
        pad = lambda a: jnp.pad(a, (0, HEAD_PAD - QK_HEAD)).reshape(1, HEAD_PAD)
        return pad(g), pad(gp)

    gqc, gqs = gain_tabs(g_qn)
    gkc, gks = gain_tabs(g_kn)
    j = jnp.arange(FNET_GROUP_W, dtype=jnp.int32)
    ang = (2.0 * np.pi / FNET_GROUP_W) * ((j[:, None] * j[None, :]) % FNET_GROUP_W).astype(F32)
    eye_g = jnp.eye(FNET_GROUPS, dtype=F32)
    bd = jnp.concatenate([jnp.kron(eye_g, jnp.cos(ang)), jnp.kron(eye_g, jnp.sin(ang))], axis=1).astype(BF16)
    row = lambda a: a.reshape(1, -1).astype(F32)
    return {
        'g_mix': row(g_mix), 'w1': w1, 'b_gate': row(b_gate), 'g_qlat': row(g_qlat),
        'g_kvlat': row(g_kvlat), 'wq': wq, 'wqr': wqr, 'wk': wk, 'wv': wv, 'vone': vone, 'ekr': ekr,
        'gqc': gqc, 'gqs': gqs, 'gkc': gkc, 'gks': gks, 'bd': bd,
        'w_a': w_a.astype(BF16), 'w_f': w_f.astype(BF16), 'w_o': w_o.astype(BF16),
        'g_ffn': row(g_ffn),
        'w_sgu': jnp.concatenate([w_s_gate, w_s_up], axis=1).astype(BF16),
        'w_sd': w_s_down.astype(BF16),
        'w_rt': w_router.T.astype(F32), 'b_r': b_router.reshape(N_EXPERTS, 1).astype(F32),
        'w_eg': w_e_gate.astype(BF16), 'w_eu': w_e_up.astype(BF16), 'w_ed': w_e_down.astype(BF16),
        'g_ple': row(g_ple), 'w_pg': w_ple_gate.astype(BF16), 'w_pp': w_ple_proj.astype(BF16),
    }


def _tables(S):
    half = QK_ROPE // 2
    freqs = 1.0 / (ROPE_THETA ** (jnp.arange(half, dtype=F32) / half))
    ang = jnp.arange(S, dtype=F32)[:, None] * freqs[None, :]
    cos, sin = jnp.cos(ang), jnp.sin(ang)
    cos_t = jnp.concatenate([jnp.ones((S, QK_NOPE), F32), cos, cos,
                             jnp.zeros((S, HEAD_PAD - QK_HEAD), F32)], axis=1)
    sin_t = jnp.concatenate([jnp.zeros((S, QK_NOPE), F32), sin, sin,
                             jnp.zeros((S, HEAD_PAD - QK_HEAD), F32)], axis=1)
    s = jnp.arange(S, dtype=jnp.int32)
    a = (2.0 * np.pi / S) * ((s[:, None] * s[None, :]) % S).astype(F32)
    norm = (S * FNET_GROUP_W) ** -0.5
    return {'cos': cos_t, 'sin': sin_t,
            'dft_c': (jnp.cos(a) * norm).astype(BF16),
            'dft_s': (-jnp.sin(a) * norm).astype(BF16)}


def _block(x, p, wts, tabs, *, tm, tq, tk, tdft, tmoe, chunk):
    B, S, D = x.shape
    T = B * S
    x2d = x.reshape(T, D)
    q, k, v, fcs, gates = _inproj(x2d, S, tabs, wts, tm)
    o = _attention(q.reshape(B, S, -1), k.reshape(B, S, -1), v.reshape(B, S, -1), tq, tk)
    bf = _seqdft(fcs.reshape(B, S, -1), tabs['dft_c'], tabs['dft_s'], tdft)
    hn, base, idx, w = _merge(x2d, o.reshape(T, -1), bf.reshape(T, -1), gates, wts, tm)
    tmoe = min(tmoe, T)
    cnt, start, tok_sorted, w_sorted = _dispatch(idx, w, tmoe)
    routed = _moe(hn, cnt, start, tok_sorted, w_sorted, wts, tmoe, chunk)
    y = _ple(base, routed, p.reshape(T, -1), wts, tm)
    return y.reshape(B, S, D)


def kernel(x_prompt, x_sample, p_prompt, p_sample, g_mix, w_in, b_gate, g_qlat, w_q_up, g_kvlat, w_kv_up, g_qn, g_kn, w_a, w_f, w_o, g_ffn, w_router, b_router, w_e_gate, w_e_up, w_e_down, w_s_gate, w_s_up, w_s_down, g_ple, w_ple_gate, w_ple_proj):
    params = (g_mix, w_in, b_gate, g_qlat, w_q_up, g_kvlat, w_kv_up, g_qn, g_kn, w_a, w_f, w_o,
              g_ffn, w_router, b_router, w_e_gate, w_e_up, w_e_down, w_s_gate, w_s_up, w_s_down,
              g_ple, w_ple_gate, w_ple_proj)
    depth = g_mix.shape[0]
    cfg = dict(tm=512, tq=512, tk=1024, tdft=512, tmoe=2048, chunk=288)
    xp, xs = x_prompt, x_sample
    tabs = _tables(x_prompt.shape[1])
    tabs_s = tabs if x_sample.shape[1] == x_prompt.shape[1] else _tables(x_sample.shape[1])
    for l in range(depth):
        wts = _prep_weights(*[a[l] for a in params])
        xp = _block(xp, p_prompt[l], wts, tabs, **cfg)
        xs = _block(xs, p_sample[l], wts, tabs_s, **cfg)
    return (xp, xs)
```

```python
import functools

import jax
import jax.numpy as jnp
import numpy as np
from jax import lax
from jax.experimental import pallas as pl
from jax.experimental.pallas import tpu as pltpu

D_MODEL = 1024
N_HEADS = 8
QK_NOPE = 64
QK_ROPE = 32
QK_HEAD = QK_NOPE + QK_ROPE
V_HEAD = 64
Q_LORA = 384
KV_LORA = 256
ROPE_THETA = 10000.0
FNET_GROUPS = 4
FNET_GROUP_W = 128
FNET_W = FNET_GROUPS * FNET_GROUP_W
N_EXPERTS = 64
TOP_K = 8
N_EXPERT_GROUPS = 8
TOPK_GROUPS = 4
EXPERT_FF = 256
SHARED_FF = 256
ROUTED_SCALE = 2.5
PLE_DIM = 256
EPS = 1e-6

LANES = 128
SUBLANES = 8
HEAD_PAD = LANES
ROWS_PER_TOKEN = D_MODEL // LANES
GROUP = 8
HEADS_PER_STEP = 2

C_Q = 0
C_KV = C_Q + Q_LORA
C_KR = C_KV + KV_LORA
C_F = C_KR + LANES
C_G = C_F + FNET_W
C_END = C_G + 2 * D_MODEL

VMEM_LIMIT = 56 * 1024 * 1024
BF16 = jnp.bfloat16
F32 = jnp.float32


def _rms(x, n):
    return lax.rsqrt(jnp.sum(x * x, axis=-1, keepdims=True) * (1.0 / n) + EPS)


def _dot(a, b):
    return jnp.dot(a, b, preferred_element_type=F32)


def _inproj_kernel(x_ref, cos_ref, sin_ref, gmix_ref, w1_ref, bg_ref, gq_ref, gkv_ref,
                   wq_ref, wqr_ref, wk_ref, wv_ref, vone_ref, ekr_ref, gqc_ref, gqs_ref, gkc_ref,
                   gks_ref, bd_ref, q_out, k_out, v_out, f_out, g_out):
    x = x_ref[...]
    h = (x * _rms(x, D_MODEL) * gmix_ref[...]).astype(BF16)
    proj = _dot(h, w1_ref[...])
    q_lat = proj[:, C_Q:C_KV]
    q_lat = (q_lat * _rms(q_lat, Q_LORA) * gq_ref[...]).astype(BF16)
    kv_lat = proj[:, C_KV:C_KR]
    kv_lat = (kv_lat * _rms(kv_lat, KV_LORA) * gkv_ref[...]).astype(BF16)
    kr = proj[:, C_KR:C_F].astype(BF16)
    f_in = proj[:, C_F:C_G].astype(BF16)
    g_out[...] = jax.nn.sigmoid(proj[:, C_G:C_END] + bg_ref[...]).astype(BF16)
    f_out[...] = _dot(f_in, bd_ref[...]).astype(BF16)
    v_out[...] = (_dot(kv_lat, wv_ref[...]) + vone_ref[...]).astype(BF16)

    q = _dot(q_lat, wq_ref[...])
    qr = _dot(q_lat, wqr_ref[...])
    kre = _dot(kr, ekr_ref[...])
    k = _dot(kv_lat, wk_ref[...]) + kre[:, :N_HEADS * HEAD_PAD]
    kr_rot = kre[:, N_HEADS * HEAD_PAD:]
    cos = cos_ref[...]
    sin = sin_ref[...]
    qc = gqc_ref[...] * cos
    qs = gqs_ref[...] * sin
    kc = gkc_ref[...] * cos
    ks = gks_ref[...] * sin
    scale = QK_HEAD ** -0.5 * np.log2(np.e)
    for hd in range(N_HEADS):
        sl = slice(hd * HEAD_PAD, (hd + 1) * HEAD_PAD)
        qh = q[:, sl]
        rq = _rms(qh, QK_HEAD) * scale
        q_out[:, sl] = ((qh * qc + qr[:, sl] * qs) * rq).astype(BF16)
        kh = k[:, sl]
        rk = _rms(kh, QK_HEAD)
        k_out[:, sl] = ((kh * kc + kr_rot[:, sl] * ks) * rk).astype(BF16)


def _inproj(x2d, S, tabs, wts, tm):
    T = x2d.shape[0]
    n_pos = S // tm
    full = lambda a: pl.BlockSpec(a.shape, lambda i: (0,) * a.ndim)
    tok = lambda w: pl.BlockSpec((tm, w), lambda i: (i, 0))
    pos = pl.BlockSpec((tm, HEAD_PAD), lambda i: (i % n_pos, 0))
    names = ['g_mix', 'w1', 'b_gate', 'g_qlat', 'g_kvlat', 'wq', 'wqr', 'wk', 'wv', 'vone', 'ekr',
             'gqc', 'gqs', 'gkc', 'gks', 'bd']
    ws = [wts[n] for n in names]
    return pl.pallas_call(
        _inproj_kernel,
        out_shape=(jax.ShapeDtypeStruct((T, N_HEADS * HEAD_PAD), BF16),
                   jax.ShapeDtypeStruct((T, N_HEADS * HEAD_PAD), BF16),
                   jax.ShapeDtypeStruct((T, N_HEADS * HEAD_PAD), BF16),
                   jax.ShapeDtypeStruct((T, 2 * FNET_W), BF16),
                   jax.ShapeDtypeStruct((T, 2 * D_MODEL), BF16)),
        grid=(T // tm,),
        in_specs=[tok(D_MODEL), pos, pos] + [full(w) for w in ws],
        out_specs=(tok(N_HEADS * HEAD_PAD), tok(N_HEADS * HEAD_PAD), tok(N_HEADS * HEAD_PAD),
                   tok(2 * FNET_W), tok(2 * D_MODEL)),
        compiler_params=pltpu.CompilerParams(dimension_semantics=("arbitrary",),
                                             vmem_limit_bytes=VMEM_LIMIT),
        name="inproj",
    )(x2d, tabs['cos'], tabs['sin'], *ws)


def _attn_kernel(q_ref, k_ref, v_ref, o_ref, s_a, s_b, *, tk):
    S = k_ref.shape[0]
    tq = q_ref.shape[0]
    nblk = tk // LANES
    n_chunks = S // tk
    heads = [slice(a * HEAD_PAD, (a + 1) * HEAD_PAD) for a in range(HEADS_PER_STEP)]
    s_bufs = (s_a, s_b)

    def scores(c, s_ref):
        for a, hs in enumerate(heads):
            s_ref[a] = lax.dot_general(q_ref[:, hs], k_ref[c * tk:(c + 1) * tk, hs],
                                       (((1,), (1,)), ((), ())), preferred_element_type=F32)

    def consume(c, s_ref, carry):
        new = []
        for a, hs in enumerate(heads):
            m, acc = carry[a]
            mx = s_ref[a, :, :LANES]
            for j in range(1, nblk):
                mx = jnp.maximum(mx, s_ref[a, :, j * LANES:(j + 1) * LANES])
            m_new = jnp.maximum(m, jnp.max(mx, axis=-1, keepdims=True))
            alpha = jnp.exp2(m - m_new)
            p = jnp.concatenate(
                [jnp.exp2((s_ref[a, :, j * LANES:(j + 1) * LANES] - m_new).astype(BF16))
                 for j in range(nblk)], axis=1)
            acc = alpha * acc + _dot(p, v_ref[c * tk:(c + 1) * tk, hs])
            new.append((m_new, acc))
        return new

    carry = [(jnp.full((tq, LANES), -jnp.inf, F32), jnp.zeros((tq, HEAD_PAD), F32))
             for _ in heads]
    scores(0, s_bufs[0])
    for c in range(n_chunks):
        if c + 1 < n_chunks:
            scores(c + 1, s_bufs[(c + 1) % 2])
        carry = consume(c, s_bufs[c % 2], carry)
    lane = lax.broadcasted_iota(jnp.int32, (tq, 2 * V_HEAD), 1)
    for pair in range(HEADS_PER_STEP // 2):
        acc_a, acc_b = carry[2 * pair][1], carry[2 * pair + 1][1]
        o_a = acc_a / pltpu.roll(acc_a, V_HEAD, 1)
        o_b = acc_b / pltpu.roll(acc_b, V_HEAD, 1)
        o_ref[:, pair * 2 * V_HEAD:(pair + 1) * 2 * V_HEAD] = jnp.where(
            lane < V_HEAD, o_a, pltpu.roll(o_b, V_HEAD, 1)).astype(o_ref.dtype)


def _attention(q, k, v, tq, tk):
    B, S, _ = q.shape
    hp = HEADS_PER_STEP
    return pl.pallas_call(
        functools.partial(_attn_kernel, tk=tk),
        out_shape=jax.ShapeDtypeStruct((B, S, N_HEADS * V_HEAD), BF16),
        grid=(B, N_HEADS // hp, S // tq),
        in_specs=[pl.BlockSpec((None, tq, hp * HEAD_PAD), lambda b, p, i: (b, i, p)),
                  pl.BlockSpec((None, S, hp * HEAD_PAD), lambda b, p, i: (b, 0, p)),
                  pl.BlockSpec((None, S, hp * HEAD_PAD), lambda b, p, i: (b, 0, p))],
        out_specs=pl.BlockSpec((None, tq, hp * V_HEAD), lambda b, p, i: (b, i, p)),
        scratch_shapes=[pltpu.VMEM((hp, tq, tk), F32), pltpu.VMEM((hp, tq, tk), F32)],
        compiler_params=pltpu.CompilerParams(
            dimension_semantics=("arbitrary", "arbitrary", "arbitrary"),
            vmem_limit_bytes=VMEM_LIMIT),
        name="attention",
    )(q, k, v)


def _seqdft_kernel(c_ref, s_ref, f_ref, o_ref):
    y = _dot(c_ref[...], f_ref[:, :FNET_W]) + _dot(s_ref[...], f_ref[:, FNET_W:])
    o_ref[...] = y.astype(o_ref.dtype)


def _seqdft(fcs, ctab, stab, tm):
    B, S, _ = fcs.shape
    return pl.pallas_call(
        _seqdft_kernel,
        out_shape=jax.ShapeDtypeStruct((B, S, FNET_W), BF16),
        grid=(S // tm, B),
        in_specs=[pl.BlockSpec((tm, S), lambda i, b: (i, 0)),
                  pl.BlockSpec((tm, S), lambda i, b: (i, 0)),
                  pl.BlockSpec((None, S, 2 * FNET_W), lambda i, b: (b, 0, 0))],
        out_specs=pl.BlockSpec((None, tm, FNET_W), lambda i, b: (b, i, 0)),
        compiler_params=pltpu.CompilerParams(dimension_semantics=("arbitrary", "arbitrary"),
                                             vmem_limit_bytes=VMEM_LIMIT),
        name="seqdft",
    )(ctab, stab, fcs)


def _first_argmax(v, iota, n):
    m = jnp.max(v, axis=0, keepdims=True)
    i = jnp.min(jnp.where(v == m, iota, n), axis=0, keepdims=True)
    return m, i


def _route(logits_t, bias):
    tm = logits_t.shape[1]
    per = N_EXPERTS // N_EXPERT_GROUPS
    scores = jax.nn.sigmoid(logits_t)
    sel = scores + bias
    neg = jnp.float32(-jnp.inf)
    iota_p = lax.broadcasted_iota(jnp.int32, (per, tm), 0)
    iota_g = lax.broadcasted_iota(jnp.int32, (N_EXPERT_GROUPS, tm), 0)
    gscore = jnp.zeros((N_EXPERT_GROUPS, tm), F32)
    for g in range(N_EXPERT_GROUPS):
        v = sel[g * per:(g + 1) * per, :]
        m1, i1 = _first_argmax(v, iota_p, per)
        m2 = jnp.max(jnp.where(iota_p == i1, neg, v), axis=0, keepdims=True)
        gscore = jnp.where(iota_g == g, m1 + m2, gscore)
    gmask = jnp.zeros((N_EXPERT_GROUPS, tm), jnp.bool_)
    v = gscore
    for _ in range(TOPK_GROUPS):
        _, i = _first_argmax(v, iota_g, N_EXPERT_GROUPS)
        hit = iota_g == i
        gmask = jnp.logical_or(gmask, hit)
        v = jnp.where(hit, neg, v)
    gkeep = jnp.where(gmask, 1.0, 0.0)
    parts = []
    for g in range(N_EXPERT_GROUPS):
        keep = gkeep[g:g + 1, :] > 0.5
        parts.append(jnp.where(keep, sel[g * per:(g + 1) * per, :], neg))
    v = jnp.concatenate(parts, axis=0)
    iota_e = lax.broadcasted_iota(jnp.int32, (N_EXPERTS, tm), 0)
    iota_k = lax.broadcasted_iota(jnp.int32, (TOP_K, tm), 0)
    idx = jnp.zeros((TOP_K, tm), jnp.int32)
    wts = jnp.zeros((TOP_K, tm), F32)
    for kk in range(TOP_K):
        _, i = _first_argmax(v, iota_e, N_EXPERTS)
        hit = iota_e == i
        wk = jnp.sum(jnp.where(hit, scores, 0.0), axis=0, keepdims=True)
        v = jnp.where(hit, neg, v)
        idx = jnp.where(iota_k == kk, i, idx)
        wts = jnp.where(iota_k == kk, wk, wts)
    wts = wts / jnp.sum(wts, axis=0, keepdims=True) * ROUTED_SCALE
    return idx, wts


def _merge_kernel(x_ref, o_ref, bf_ref, g_ref, wa_ref, wf_ref, wo_ref, gffn_ref, wsgu_ref,
                  wsd_ref, wrt_ref, br_ref, hn_out, base_out, idx_out, w_out):
    a = _dot(o_ref[...], wa_ref[...])
    b = _dot(bf_ref[...], wf_ref[...])
    g = g_ref[...].astype(F32)
    m = (g[:, :D_MODEL] * a + g[:, D_MODEL:] * b).astype(BF16)
    x2 = x_ref[...] + _dot(m, wo_ref[...])
    hn = x2 * _rms(x2, D_MODEL) * gffn_ref[...]
    tm = hn.shape[0]
    for r in range(ROWS_PER_TOKEN):
        hn_out[pl.ds(r, tm, stride=ROWS_PER_TOKEN), :] = hn[:, r * LANES:(r + 1) * LANES]
    hb = hn.astype(BF16)
    gu = _dot(hb, wsgu_ref[...])
    act = (jax.nn.silu(gu[:, :SHARED_FF]) * gu[:, SHARED_FF:]).astype(BF16)
    base_out[...] = x2 + _dot(act, wsd_ref[...])
    logits_t = lax.dot_general(wrt_ref[...], hn, (((1,), (1,)), ((), ())),
                               precision=lax.Precision.HIGHEST, preferred_element_type=F32)
    idx, wts = _route(logits_t, br_ref[...])
    idx_out[...] = idx
    w_out[...] = wts


def _merge(x2d, o2d, bf2d, gates, wts, tm):
    T = x2d.shape[0]
    full = lambda a: pl.BlockSpec(a.shape, lambda i: (0,) * a.ndim)
    tok = lambda w: pl.BlockSpec((tm, w), lambda i: (i, 0))
    names = ['w_a', 'w_f', 'w_o', 'g_ffn', 'w_sgu', 'w_sd', 'w_rt', 'b_r']
    ws = [wts[n] for n in names]
    return pl.pallas_call(
        _merge_kernel,
        out_shape=(jax.ShapeDtypeStruct((T * ROWS_PER_TOKEN, LANES), F32),
                   jax.ShapeDtypeStruct((T, D_MODEL), F32),
                   jax.ShapeDtypeStruct((TOP_K, T), jnp.int32),
                   jax.ShapeDtypeStruct((TOP_K, T), F32)),
        grid=(T // tm,),
        in_specs=[tok(D_MODEL), tok(N_HEADS * V_HEAD), tok(FNET_W), tok(2 * D_MODEL)]
                 + [full(w) for w in ws],
        out_specs=(pl.BlockSpec((tm * ROWS_PER_TOKEN, LANES), lambda i: (i, 0)), tok(D_MODEL),
                   pl.BlockSpec((TOP_K, tm), lambda i: (0, i)),
                   pl.BlockSpec((TOP_K, tm), lambda i: (0, i))),
        compiler_params=pltpu.CompilerParams(dimension_semantics=("arbitrary",),
                                             vmem_limit_bytes=VMEM_LIMIT),
        name="merge",
    )(x2d, o2d, bf2d, gates, *ws)


def _moe_kernel(cnt_ref, start_ref, tok_hbm, w_hbm, hn_ref, wg_ref, wu_ref, wd_ref, out_ref,
                tok_s, w_s, g0, g1, g2, y0, y1, y2, sem, *, chunk, static_scatter):
    i = pl.program_id(0)
    e = pl.program_id(1)
    R = ROWS_PER_TOKEN
    step = i * N_EXPERTS + e

    def tile_at(row):
        return pl.ds(pl.multiple_of(row * R, R), R)

    def gather_group(base, j0, gbuf):
        vals = [hn_ref[tok_s[base + j0 + r]] for r in range(GROUP)]
        for r in range(GROUP):
            gbuf[tile_at(j0 + r), :] = vals[r]

    def scatter_group(base, j0, ybuf):
        toks = [tok_s[base + j0 + r] for r in range(GROUP)]
        vals = [out_ref[toks[r]] + w_s[base + j0 + r] * ybuf[tile_at(j0 + r), :]
                for r in range(GROUP)]
        for r in range(GROUP):
            out_ref[toks[r]] = vals[r]

    def gather_rolled(base, rows, gbuf):
        def group(g, _):
            gather_group(base, g * GROUP, gbuf)
            return 0

        def row(j, _):
            gbuf[tile_at(j), :] = hn_ref[tok_s[base + j]]
            return 0

        lax.fori_loop(0, rows // GROUP, group, 0)
        lax.fori_loop(rows // GROUP * GROUP, rows, row, 0)

    def scatter_rolled(base, first_row, rows, ybuf):
        def group(g, _):
            scatter_group(base, g * GROUP, ybuf)
            return 0

        def row(j, _):
            t = tok_s[base + j]
            out_ref[t] = out_ref[t] + w_s[base + j] * ybuf[tile_at(j), :]
            return 0

        lax.fori_loop(first_row // GROUP, rows // GROUP, group, 0)
        lax.fori_loop(jnp.maximum(rows // GROUP * GROUP, first_row), rows, row, 0)

    def ffn(gbuf, ybuf):
        xg = jnp.concatenate([gbuf[pl.ds(r, chunk, stride=R), :] for r in range(R)],
                             axis=1).astype(BF16)
        g = _dot(xg, wg_ref[0])
        u = _dot(xg, wu_ref[0])
        act = (jax.nn.silu(g) * u).astype(BF16)
        y = _dot(act, wd_ref[0])
        for r in range(R):
            ybuf[pl.ds(r, chunk, stride=R), :] = y[:, r * LANES:(r + 1) * LANES]

    n = cnt_ref[step]
    st = start_ref[step]
    rows = jnp.minimum(n, chunk)
    st_next = start_ref[step + 1]
    prev = jnp.maximum(step - 1, 0)
    st_prev = start_ref[prev]
    rows_prev = jnp.where(e > 0, jnp.minimum(cnt_ref[prev], chunk), 0)

    @pl.when(step == 0)
    def _():
        for buf in (g0, g1, g2):
            buf[...] = jnp.zeros_like(buf)

    @pl.when(e == 0)
    def _():
        cp_t = pltpu.make_async_copy(tok_hbm.at[i], tok_s, sem.at[0])
        cp_w = pltpu.make_async_copy(w_hbm.at[i], w_s, sem.at[1])
        cp_t.start()
        cp_w.start()
        out_ref[...] = jnp.zeros_like(out_ref)
        cp_t.wait()
        cp_w.wait()
        gather_rolled(st, rows, g0)

    def pipelined_step(g_cur, g_next, y_cur, y_prev):
        @pl.when(rows_prev >= static_scatter)
        def _():
            for j0 in range(0, chunk, GROUP):
                gather_group(st_next, j0, g_next)
            ffn(g_cur, y_cur)
            for j0 in range(0, static_scatter, GROUP):
                scatter_group(st_prev, j0, y_prev)
            scatter_rolled(st_prev, static_scatter, rows_prev, y_prev)

        @pl.when(rows_prev < static_scatter)
        def _():
            gather_rolled(st_next, chunk, g_next)
            ffn(g_cur, y_cur)
            scatter_rolled(st_prev, 0, rows_prev, y_prev)

        @pl.when(e == N_EXPERTS - 1)
        def _():
            scatter_rolled(st, 0, rows, y_cur)

    @pl.when(e % 2 == 0)
    def _():
        pipelined_step(g0, g1, y0, y1)

    @pl.when(e % 2 == 1)
    def _():
        pipelined_step(g1, g0, y1, y0)

    def extra_chunk(c, _):
        base = st + c * chunk
        rows_c = jnp.minimum(chunk, n - c * chunk)
        gather_rolled(base, rows_c, g2)
        ffn(g2, y2)
        scatter_rolled(base, 0, rows_c, y2)
        return 0

    lax.fori_loop(1, pl.cdiv(n, chunk), extra_chunk, 0)


def _moe(hn_tiles, cnt, start, tok_sorted, w_sorted, wts, tmoe, chunk):
    R = ROWS_PER_TOKEN
    T = hn_tiles.shape[0] // R
    n_tiles = T // tmoe
    L = tok_sorted.shape[1]
    wspec = lambda a: pl.BlockSpec((1,) + a.shape[1:], lambda i, e, *_: (e, 0, 0))
    tile_spec = pl.BlockSpec((tmoe, R, LANES), lambda i, e, *_: (i, 0, 0))
    row_buf = pltpu.VMEM((chunk * R, LANES), F32)
    static_scatter = (chunk * 3 // 4) // GROUP * GROUP
    out = pl.pallas_call(
        functools.partial(_moe_kernel, chunk=chunk, static_scatter=static_scatter),
        out_shape=jax.ShapeDtypeStruct((T, R, LANES), F32),
        grid_spec=pltpu.PrefetchScalarGridSpec(
            num_scalar_prefetch=2,
            grid=(n_tiles, N_EXPERTS),
            in_specs=[pl.BlockSpec(memory_space=pl.ANY),
                      pl.BlockSpec(memory_space=pl.ANY),
                      tile_spec,
                      wspec(wts['w_eg']), wspec(wts['w_eu']), wspec(wts['w_ed'])],
            out_specs=tile_spec,
            scratch_shapes=[pltpu.SMEM((L,), jnp.int32),
                            pltpu.SMEM((L,), F32),
                            row_buf, row_buf, row_buf, row_buf, row_buf, row_buf,
                            pltpu.SemaphoreType.DMA((2,))]),
        compiler_params=pltpu.CompilerParams(dimension_semantics=("arbitrary", "arbitrary"),
                                             vmem_limit_bytes=VMEM_LIMIT),
        name="moe",
    )(cnt, start, tok_sorted, w_sorted, hn_tiles.reshape(T, R, LANES),
      wts['w_eg'], wts['w_eu'], wts['w_ed'])
    return out.reshape(T * R, LANES)


def _dispatch(idx, w, tmoe, chunk):
    T = idx.shape[1]
    n_tiles = T // tmoe
    per_tile = lambda a: a.reshape(TOP_K, n_tiles, tmoe).transpose(1, 0, 2).reshape(n_tiles, TOP_K * tmoe)
    e_t, w_t = per_tile(idx), per_tile(w)
    tok_l = jnp.broadcast_to(jnp.tile(jnp.arange(tmoe, dtype=jnp.int32), TOP_K)[None, :], e_t.shape)
    _, tok_sorted, w_sorted = lax.sort((e_t, tok_l, w_t), dimension=1, num_keys=1)
    onehot = e_t[..., None] == jnp.arange(N_EXPERTS, dtype=jnp.int32)
    cnt = jnp.sum(onehot.astype(jnp.int32), axis=1)
    start = jnp.cumsum(cnt, axis=1) - cnt
    pad_list = lambda a: jnp.pad(a, ((0, 0), (0, chunk)))
    pad_step = lambda a: jnp.pad(a.reshape(-1), (0, 1))
    return pad_step(cnt), pad_step(start), pad_list(tok_sorted), pad_list(w_sorted)


def _ple_kernel(base_ref, routed_ref, p_ref, gple_ref, wg_ref, wp_ref, y_ref):
    tm = base_ref.shape[0]
    routed = jnp.concatenate([routed_ref[pl.ds(r, tm, stride=ROWS_PER_TOKEN), :]
                              for r in range(ROWS_PER_TOKEN)], axis=1)
    x3 = base_ref[...] + routed
    hn = (x3 * _rms(x3, D_MODEL) * gple_ref[...]).astype(BF16)
    gate = jax.nn.sigmoid(_dot(hn, wg_ref[...]))
    y_ref[...] = x3 + gate * _dot(p_ref[...].astype(BF16), wp_ref[...])


def _ple(base, routed, p2d, wts, tm):
    T = base.shape[0]
    full = lambda a: pl.BlockSpec(a.shape, lambda i: (0,) * a.ndim)
    tok = lambda w: pl.BlockSpec((tm, w), lambda i: (i, 0))
    ws = [wts['g_ple'], wts['w_pg'], wts['w_pp']]
    return pl.pallas_call(
        _ple_kernel,
        out_shape=jax.ShapeDtypeStruct((T, D_MODEL), F32),
        grid=(T // tm,),
        in_specs=[tok(D_MODEL), pl.BlockSpec((tm * ROWS_PER_TOKEN, LANES), lambda i: (i, 0)),
                  tok(PLE_DIM)] + [full(w) for w in ws],
        out_specs=tok(D_MODEL),
        compiler_params=pltpu.CompilerParams(dimension_semantics=("arbitrary",),
                                             vmem_limit_bytes=VMEM_LIMIT),
        name="ple",
    )(base, routed, p2d, *ws)


def _head_pad_cols(w, width):
    K = w.shape[0]
    w = w.reshape(K, N_HEADS, width)
    return jnp.pad(w, ((0, 0), (0, 0), (0, HEAD_PAD - width))).reshape(K, N_HEADS * HEAD_PAD)


def _rot_partner(w):
    half = QK_ROPE // 2
    z = jnp.zeros_like(w[..., :QK_NOPE])
    return jnp.concatenate([z, -w[..., QK_NOPE + half:QK_HEAD], w[..., QK_NOPE:QK_NOPE + half]], axis=-1)


def _prep_weights(g_mix, w_in, b_gate, g_qlat, w_q_up, g_kvlat, w_kv_up, g_qn, g_kn, w_a, w_f,
                  w_o, g_ffn, w_router, b_router, w_e_gate, w_e_up, w_e_down, w_s_gate, w_s_up,
                  w_s_down, g_ple, w_ple_gate, w_ple_proj):
    half = QK_ROPE // 2
    off_kv, off_kr = Q_LORA, Q_LORA + KV_LORA
    off_f = off_kr + QK_ROPE
    off_g = off_f + FNET_W
    w1 = jnp.concatenate([
        w_in[:, :off_kr],
        jnp.pad(w_in[:, off_kr:off_f], ((0, 0), (0, LANES - QK_ROPE))),
        w_in[:, off_f:]], axis=1).astype(BF16)
    wq3 = w_q_up.reshape(Q_LORA, N_HEADS, QK_HEAD)
    wq = _head_pad_cols(w_q_up, QK_HEAD).astype(BF16)
    wqr = _head_pad_cols(_rot_partner(wq3).reshape(Q_LORA, -1), QK_HEAD).astype(BF16)
    wkv3 = w_kv_up.reshape(KV_LORA, N_HEADS, QK_NOPE + V_HEAD)
    wk = _head_pad_cols(wkv3[:, :, :QK_NOPE].reshape(KV_LORA, -1), QK_NOPE).astype(BF16)
    wv = _head_pad_cols(wkv3[:, :, QK_NOPE:].reshape(KV_LORA, -1), V_HEAD).astype(BF16)
    vone = jnp.tile(jnp.concatenate([jnp.zeros((V_HEAD,), F32), jnp.ones((HEAD_PAD - V_HEAD,), F32)]),
                    N_HEADS).reshape(1, N_HEADS * HEAD_PAD)
    eye = jnp.eye(QK_ROPE, dtype=F32)
    place = jnp.concatenate([jnp.zeros((QK_ROPE, QK_NOPE), F32), eye], axis=1)
    place_rot = _rot_partner(place)
    ekr = jnp.concatenate([
        _head_pad_cols(jnp.tile(place, (1, N_HEADS)), QK_HEAD),
        _head_pad_cols(jnp.tile(place_rot, (1, N_HEADS)), QK_HEAD)], axis=1)
    ekr = jnp.pad(ekr, ((0, LANES - QK_ROPE), (0, 0))).astype(BF16)

    def gain_tabs(g):
        gp = jnp.concatenate([jnp.zeros((QK_NOPE,), F32), g[QK_NOPE + half:], g[QK_NOPE:QK_NOPE + half]])
        pad = lambda a: jnp.pad(a, (0, HEAD_PAD - QK_HEAD)).reshape(1, HEAD_PAD)
        return pad(g), pad(gp)

    gqc, gqs = gain_tabs(g_qn)
    gkc, gks = gain_tabs(g_kn)
    j = jnp.arange(FNET_GROUP_W, dtype=jnp.int32)
    ang = (2.0 * np.pi / FNET_GROUP_W) * ((j[:, None] * j[None, :]) % FNET_GROUP_W).astype(F32)
    eye_g = jnp.eye(FNET_GROUPS, dtype=F32)
    bd = jnp.concatenate([jnp.kron(eye_g, jnp.cos(ang)), jnp.kron(eye_g, jnp.sin(ang))], axis=1).astype(BF16)
    row = lambda a: a.reshape(1, -1).astype(F32)
    return {
        'g_mix': row(g_mix), 'w1': w1, 'b_gate': row(b_gate), 'g_qlat': row(g_qlat),
        'g_kvlat': row(g_kvlat), 'wq': wq, 'wqr': wqr, 'wk': wk, 'wv': wv, 'vone': vone, 'ekr': ekr,
        'gqc': gqc, 'gqs': gqs, 'gkc': gkc, 'gks': gks, 'bd': bd,
        'w_a': w_a.astype(BF16), 'w_f': w_f.astype(BF16), 'w_o': w_o.astype(BF16),
        'g_ffn': row(g_ffn),
        'w_sgu': jnp.concatenate([w_s_gate, w_s_up], axis=1).astype(BF16),
        'w_sd': w_s_down.astype(BF16),
        'w_rt': w_router.T.astype(F32), 'b_r': b_router.reshape(N_EXPERTS, 1).astype(F32),
        'w_eg': w_e_gate.astype(BF16), 'w_eu': w_e_up.astype(BF16), 'w_ed': w_e_down.astype(BF16),
        'g_ple': row(g_ple), 'w_pg': w_ple_gate.astype(BF16), 'w_pp': w_ple_proj.astype(BF16),
    }


def _tables(S):
    half = QK_ROPE // 2
    freqs = 1.0 / (ROPE_THETA ** (jnp.arange(half, dtype=F32) / half))
    ang = jnp.arange(S, dtype=F32)[:, None] * freqs[None, :]
    cos, sin = jnp.cos(ang), jnp.sin(ang)
    cos_t = jnp.concatenate([jnp.ones((S, QK_NOPE), F32), cos, cos,
                             jnp.zeros((S, HEAD_PAD - QK_HEAD), F32)], axis=1)
    sin_t = jnp.concatenate([jnp.zeros((S, QK_NOPE), F32), sin, sin,
                             jnp.zeros((S, HEAD_PAD - QK_HEAD), F32)], axis=1)
    s = jnp.arange(S, dtype=jnp.int32)
    a = (2.0 * np.pi / S) * ((s[:, None] * s[None, :]) % S).astype(F32)
    norm = (S * FNET_GROUP_W) ** -0.5
    return {'cos': cos_t, 'sin': sin_t,
            'dft_c': (jnp.cos(a) * norm).astype(BF16),
            'dft_s': (-jnp.sin(a) * norm).astype(BF16)}


def _block(x, p, wts, tabs, *, tm, tq, tk, tdft, tmoe, chunk):
    B, S, D = x.shape
    T = B * S
    x2d = x.reshape(T, D)
    q, k, v, fcs, gates = _inproj(x2d, S, tabs, wts, tm)
    o = _attention(q.reshape(B, S, -1), k.reshape(B, S, -1), v.reshape(B, S, -1), tq, tk)
    bf = _seqdft(fcs.reshape(B, S, -1), tabs['dft_c'], tabs['dft_s'], tdft)
    hn, base, idx, w = _merge(x2d, o.reshape(T, -1), bf.reshape(T, -1), gates, wts, tm)
    tmoe = min(tmoe, T)
    cnt, start, tok_sorted, w_sorted = _dispatch(idx, w, tmoe, chunk)
    routed = _moe(hn, cnt, start, tok_sorted, w_sorted, wts, tmoe, chunk)
    y = _ple(base, routed, p.reshape(T, -1), wts, tm)
    return y.reshape(B, S, D)


def kernel(x_prompt, x_sample, p_prompt, p_sample, g_mix, w_in, b_gate, g_qlat, w_q_up, g_kvlat, w_kv_up, g_qn, g_kn, w_a, w_f, w_o, g_ffn, w_router, b_router, w_e_gate, w_e_up, w_e_down, w_s_gate, w_s_up, w_s_down, g_ple, w_ple_gate, w_ple_proj):
    params = (g_mix, w_in, b_gate, g_qlat, w_q_up, g_kvlat, w_kv_up, g_qn, g_kn, w_a, w_f, w_o,
              g_ffn, w_router, b_router, w_e_gate, w_e_up, w_e_down, w_s_gate, w_s_up, w_s_down,
              g_ple, w_ple_gate, w_ple_proj)
    depth = g_mix.shape[0]
    cfg = dict(tm=512, tq=512, tk=1024, tdft=512, tmoe=2048, chunk=288)
    xp, xs = x_prompt, x_sample
    tabs = _tables(x_prompt.shape[1])
    tabs_s = tabs if x_sample.shape[1] == x_prompt.shape[1] else _tables(x_sample.shape[1])
    for l in range(depth):
        wts = _prep_weights(*[a[l] for a in params])
        xp = _block(xp, p_prompt[l], wts, tabs, **cfg)
        xs = _block(xs, p_sample[l], wts, tabs_s, **cfg)
    return (xp, xs)
```

```python
import functools

import jax
import jax.numpy as jnp
import numpy as np
from jax import lax
from jax.experimental import pallas as pl
from jax.experimental.pallas import tpu as pltpu

D_MODEL = 1024
N_HEADS = 8
QK_NOPE = 64
QK_ROPE = 32
QK_HEAD = QK_NOPE + QK_ROPE
V_HEAD = 64
Q_LORA = 384
KV_LORA = 256
ROPE_THETA = 10000.0
FNET_GROUPS = 4
FNET_GROUP_W = 128
FNET_W = FNET_GROUPS * FNET_GROUP_W
N_EXPERTS = 64
TOP_K = 8
N_EXPERT_GROUPS = 8
TOPK_GROUPS = 4
EXPERT_FF = 256
SHARED_FF = 256
ROUTED_SCALE = 2.5
PLE_DIM = 256
EPS = 1e-6

LANES = 128
SUBLANES = 8
HEAD_PAD = LANES
ROWS_PER_TOKEN = D_MODEL // LANES
GROUP = 8
EXPERTS_PER_STEP = 2
HEADS_PER_STEP = 2

C_Q = 0
C_KV = C_Q + Q_LORA
C_KR = C_KV + KV_LORA
C_F = C_KR + LANES
C_G = C_F + FNET_W
C_END = C_G + 2 * D_MODEL

VMEM_LIMIT = 56 * 1024 * 1024
BF16 = jnp.bfloat16
F32 = jnp.float32


def _rms(x, n):
    return lax.rsqrt(jnp.sum(x * x, axis=-1, keepdims=True) * (1.0 / n) + EPS)


def _dot(a, b):
    return jnp.dot(a, b, preferred_element_type=F32)


def _inproj_kernel(x_ref, cos_ref, sin_ref, gmix_ref, w1_ref, bg_ref, gq_ref, gkv_ref,
                   wq_ref, wqr_ref, wk_ref, wv_ref, vone_ref, ekr_ref, gqc_ref, gqs_ref, gkc_ref,
                   gks_ref, bd_ref, q_out, k_out, v_out, f_out, g_out):
    x = x_ref[...]
    h = (x * _rms(x, D_MODEL) * gmix_ref[...]).astype(BF16)
    proj = _dot(h, w1_ref[...])
    q_lat = proj[:, C_Q:C_KV]
    q_lat = (q_lat * _rms(q_lat, Q_LORA) * gq_ref[...]).astype(BF16)
    kv_lat = proj[:, C_KV:C_KR]
    kv_lat = (kv_lat * _rms(kv_lat, KV_LORA) * gkv_ref[...]).astype(BF16)
    kr = proj[:, C_KR:C_F].astype(BF16)
    f_in = proj[:, C_F:C_G].astype(BF16)
    g_out[...] = jax.nn.sigmoid(proj[:, C_G:C_END] + bg_ref[...]).astype(BF16)
    f_out[...] = _dot(f_in, bd_ref[...]).astype(BF16)
    v_out[...] = (_dot(kv_lat, wv_ref[...]) + vone_ref[...]).astype(BF16)

    q = _dot(q_lat, wq_ref[...])
    qr = _dot(q_lat, wqr_ref[...])
    kre = _dot(kr, ekr_ref[...])
    k = _dot(kv_lat, wk_ref[...]) + kre[:, :N_HEADS * HEAD_PAD]
    kr_rot = kre[:, N_HEADS * HEAD_PAD:]
    cos = cos_ref[...]
    sin = sin_ref[...]
    qc = gqc_ref[...] * cos
    qs = gqs_ref[...] * sin
    kc = gkc_ref[...] * cos
    ks = gks_ref[...] * sin
    scale = QK_HEAD ** -0.5 * np.log2(np.e)
    for hd in range(N_HEADS):
        sl = slice(hd * HEAD_PAD, (hd + 1) * HEAD_PAD)
        qh = q[:, sl]
        rq = _rms(qh, QK_HEAD) * scale
        q_out[:, sl] = ((qh * qc + qr[:, sl] * qs) * rq).astype(BF16)
        kh = k[:, sl]
        rk = _rms(kh, QK_HEAD)
        k_out[:, sl] = ((kh * kc + kr_rot[:, sl] * ks) * rk).astype(BF16)


def _inproj(x2d, S, tabs, wts, tm):
    T = x2d.shape[0]
    n_pos = S // tm
    full = lambda a: pl.BlockSpec(a.shape, lambda i: (0,) * a.ndim)
    tok = lambda w: pl.BlockSpec((tm, w), lambda i: (i, 0))
    pos = pl.BlockSpec((tm, HEAD_PAD), lambda i: (i % n_pos, 0))
    names = ['g_mix', 'w1', 'b_gate', 'g_qlat', 'g_kvlat', 'wq', 'wqr', 'wk', 'wv', 'vone', 'ekr',
             'gqc', 'gqs', 'gkc', 'gks', 'bd']
    ws = [wts[n] for n in names]
    return pl.pallas_call(
        _inproj_kernel,
        out_shape=(jax.ShapeDtypeStruct((T, N_HEADS * HEAD_PAD), BF16),
                   jax.ShapeDtypeStruct((T, N_HEADS * HEAD_PAD), BF16),
                   jax.ShapeDtypeStruct((T, N_HEADS * HEAD_PAD), BF16),
                   jax.ShapeDtypeStruct((T, 2 * FNET_W), BF16),
                   jax.ShapeDtypeStruct((T, 2 * D_MODEL), BF16)),
        grid=(T // tm,),
        in_specs=[tok(D_MODEL), pos, pos] + [full(w) for w in ws],
        out_specs=(tok(N_HEADS * HEAD_PAD), tok(N_HEADS * HEAD_PAD), tok(N_HEADS * HEAD_PAD),
                   tok(2 * FNET_W), tok(2 * D_MODEL)),
        compiler_params=pltpu.CompilerParams(dimension_semantics=("arbitrary",),
                                             vmem_limit_bytes=VMEM_LIMIT),
        name="inproj",
    )(x2d, tabs['cos'], tabs['sin'], *ws)


def _attn_kernel(q_ref, k_ref, v_ref, o_ref, s_a, s_b, *, tk):
    S = k_ref.shape[0]
    tq = q_ref.shape[0]
    nblk = tk // LANES
    n_chunks = S // tk
    heads = [slice(a * HEAD_PAD, (a + 1) * HEAD_PAD) for a in range(HEADS_PER_STEP)]
    s_bufs = (s_a, s_b)

    def scores(c, s_ref):
        for a, hs in enumerate(heads):
            s_ref[a] = lax.dot_general(q_ref[:, hs], k_ref[c * tk:(c + 1) * tk, hs],
                                       (((1,), (1,)), ((), ())), preferred_element_type=F32)

    def consume(c, s_ref, carry):
        new = []
        for a, hs in enumerate(heads):
            m, acc = carry[a]
            mx = s_ref[a, :, :LANES]
            for j in range(1, nblk):
                mx = jnp.maximum(mx, s_ref[a, :, j * LANES:(j + 1) * LANES])
            m_new = jnp.maximum(m, jnp.max(mx, axis=-1, keepdims=True))
            alpha = jnp.exp2(m - m_new)
            p = jnp.concatenate(
                [jnp.exp2((s_ref[a, :, j * LANES:(j + 1) * LANES] - m_new).astype(BF16))
                 for j in range(nblk)], axis=1)
            acc = alpha * acc + _dot(p, v_ref[c * tk:(c + 1) * tk, hs])
            new.append((m_new, acc))
        return new

    carry = [(jnp.full((tq, LANES), -jnp.inf, F32), jnp.zeros((tq, HEAD_PAD), F32))
             for _ in heads]
    scores(0, s_bufs[0])
    for c in range(n_chunks):
        if c + 1 < n_chunks:
            scores(c + 1, s_bufs[(c + 1) % 2])
        carry = consume(c, s_bufs[c % 2], carry)
    lane = lax.broadcasted_iota(jnp.int32, (tq, 2 * V_HEAD), 1)
    for pair in range(HEADS_PER_STEP // 2):
        acc_a, acc_b = carry[2 * pair][1], carry[2 * pair + 1][1]
        o_a = acc_a / pltpu.roll(acc_a, V_HEAD, 1)
        o_b = acc_b / pltpu.roll(acc_b, V_HEAD, 1)
        o_ref[:, pair * 2 * V_HEAD:(pair + 1) * 2 * V_HEAD] = jnp.where(
            lane < V_HEAD, o_a, pltpu.roll(o_b, V_HEAD, 1)).astype(o_ref.dtype)


def _attention(q, k, v, tq, tk):
    B, S, _ = q.shape
    hp = HEADS_PER_STEP
    return pl.pallas_call(
        functools.partial(_attn_kernel, tk=tk),
        out_shape=jax.ShapeDtypeStruct((B, S, N_HEADS * V_HEAD), BF16),
        grid=(B, N_HEADS // hp, S // tq),
        in_specs=[pl.BlockSpec((None, tq, hp * HEAD_PAD), lambda b, p, i: (b, i, p)),
                  pl.BlockSpec((None, S, hp * HEAD_PAD), lambda b, p, i: (b, 0, p)),
                  pl.BlockSpec((None, S, hp * HEAD_PAD), lambda b, p, i: (b, 0, p))],
        out_specs=pl.BlockSpec((None, tq, hp * V_HEAD), lambda b, p, i: (b, i, p)),
        scratch_shapes=[pltpu.VMEM((hp, tq, tk), F32), pltpu.VMEM((hp, tq, tk), F32)],
        compiler_params=pltpu.CompilerParams(
            dimension_semantics=("arbitrary", "arbitrary", "arbitrary"),
            vmem_limit_bytes=VMEM_LIMIT),
        name="attention",
    )(q, k, v)


def _seqdft_kernel(c_ref, s_ref, f_ref, o_ref):
    y = _dot(c_ref[...], f_ref[:, :FNET_W]) + _dot(s_ref[...], f_ref[:, FNET_W:])
    o_ref[...] = y.astype(o_ref.dtype)


def _seqdft(fcs, ctab, stab, tm):
    B, S, _ = fcs.shape
    return pl.pallas_call(
        _seqdft_kernel,
        out_shape=jax.ShapeDtypeStruct((B, S, FNET_W), BF16),
        grid=(S // tm, B),
        in_specs=[pl.BlockSpec((tm, S), lambda i, b: (i, 0)),
                  pl.BlockSpec((tm, S), lambda i, b: (i, 0)),
                  pl.BlockSpec((None, S, 2 * FNET_W), lambda i, b: (b, 0, 0))],
        out_specs=pl.BlockSpec((None, tm, FNET_W), lambda i, b: (b, i, 0)),
        compiler_params=pltpu.CompilerParams(dimension_semantics=("arbitrary", "arbitrary"),
                                             vmem_limit_bytes=VMEM_LIMIT),
        name="seqdft",
    )(ctab, stab, fcs)


def _first_argmax(v, iota, n):
    m = jnp.max(v, axis=0, keepdims=True)
    i = jnp.min(jnp.where(v == m, iota, n), axis=0, keepdims=True)
    return m, i


def _route(logits_t, bias):
    tm = logits_t.shape[1]
    per = N_EXPERTS // N_EXPERT_GROUPS
    scores = jax.nn.sigmoid(logits_t)
    sel = scores + bias
    neg = jnp.float32(-jnp.inf)
    iota_p = lax.broadcasted_iota(jnp.int32, (per, tm), 0)
    iota_g = lax.broadcasted_iota(jnp.int32, (N_EXPERT_GROUPS, tm), 0)
    gscore = jnp.zeros((N_EXPERT_GROUPS, tm), F32)
    for g in range(N_EXPERT_GROUPS):
        v = sel[g * per:(g + 1) * per, :]
        m1, i1 = _first_argmax(v, iota_p, per)
        m2 = jnp.max(jnp.where(iota_p == i1, neg, v), axis=0, keepdims=True)
        gscore = jnp.where(iota_g == g, m1 + m2, gscore)
    gmask = jnp.zeros((N_EXPERT_GROUPS, tm), jnp.bool_)
    v = gscore
    for _ in range(TOPK_GROUPS):
        _, i = _first_argmax(v, iota_g, N_EXPERT_GROUPS)
        hit = iota_g == i
        gmask = jnp.logical_or(gmask, hit)
        v = jnp.where(hit, neg, v)
    gkeep = jnp.where(gmask, 1.0, 0.0)
    parts = []
    for g in range(N_EXPERT_GROUPS):
        keep = gkeep[g:g + 1, :] > 0.5
        parts.append(jnp.where(keep, sel[g * per:(g + 1) * per, :], neg))
    v = jnp.concatenate(parts, axis=0)
    iota_e = lax.broadcasted_iota(jnp.int32, (N_EXPERTS, tm), 0)
    iota_k = lax.broadcasted_iota(jnp.int32, (TOP_K, tm), 0)
    idx = jnp.zeros((TOP_K, tm), jnp.int32)
    wts = jnp.zeros((TOP_K, tm), F32)
    for kk in range(TOP_K):
        _, i = _first_argmax(v, iota_e, N_EXPERTS)
        hit = iota_e == i
        wk = jnp.sum(jnp.where(hit, scores, 0.0), axis=0, keepdims=True)
        v = jnp.where(hit, neg, v)
        idx = jnp.where(iota_k == kk, i, idx)
        wts = jnp.where(iota_k == kk, wk, wts)
    wts = wts / jnp.sum(wts, axis=0, keepdims=True) * ROUTED_SCALE
    return idx, wts


def _merge_kernel(x_ref, o_ref, bf_ref, g_ref, wa_ref, wf_ref, wo_ref, gffn_ref, wsgu_ref,
                  wsd_ref, wrt_ref, br_ref, hn_out, base_out, idx_out, w_out):
    a = _dot(o_ref[...], wa_ref[...])
    b = _dot(bf_ref[...], wf_ref[...])
    g = g_ref[...].astype(F32)
    m = (g[:, :D_MODEL] * a + g[:, D_MODEL:] * b).astype(BF16)
    x2 = x_ref[...] + _dot(m, wo_ref[...])
    hn = x2 * _rms(x2, D_MODEL) * gffn_ref[...]
    tm = hn.shape[0]
    for r in range(ROWS_PER_TOKEN):
        hn_out[pl.ds(r, tm, stride=ROWS_PER_TOKEN), :] = hn[:, r * LANES:(r + 1) * LANES]
    hb = hn.astype(BF16)
    gu = _dot(hb, wsgu_ref[...])
    act = (jax.nn.silu(gu[:, :SHARED_FF]) * gu[:, SHARED_FF:]).astype(BF16)
    base_out[...] = x2 + _dot(act, wsd_ref[...])
    logits_t = lax.dot_general(wrt_ref[...], hn, (((1,), (1,)), ((), ())),
                               precision=lax.Precision.HIGHEST, preferred_element_type=F32)
    idx, wts = _route(logits_t, br_ref[...])
    idx_out[...] = idx
    w_out[...] = wts


def _merge(x2d, o2d, bf2d, gates, wts, tm):
    T = x2d.shape[0]
    full = lambda a: pl.BlockSpec(a.shape, lambda i: (0,) * a.ndim)
    tok = lambda w: pl.BlockSpec((tm, w), lambda i: (i, 0))
    names = ['w_a', 'w_f', 'w_o', 'g_ffn', 'w_sgu', 'w_sd', 'w_rt', 'b_r']
    ws = [wts[n] for n in names]
    return pl.pallas_call(
        _merge_kernel,
        out_shape=(jax.ShapeDtypeStruct((T * ROWS_PER_TOKEN, LANES), F32),
                   jax.ShapeDtypeStruct((T, D_MODEL), F32),
                   jax.ShapeDtypeStruct((TOP_K, T), jnp.int32),
                   jax.ShapeDtypeStruct((TOP_K, T), F32)),
        grid=(T // tm,),
        in_specs=[tok(D_MODEL), tok(N_HEADS * V_HEAD), tok(FNET_W), tok(2 * D_MODEL)]
                 + [full(w) for w in ws],
        out_specs=(pl.BlockSpec((tm * ROWS_PER_TOKEN, LANES), lambda i: (i, 0)), tok(D_MODEL),
                   pl.BlockSpec((TOP_K, tm), lambda i: (0, i)),
                   pl.BlockSpec((TOP_K, tm), lambda i: (0, i))),
        compiler_params=pltpu.CompilerParams(dimension_semantics=("arbitrary",),
                                             vmem_limit_bytes=VMEM_LIMIT),
        name="merge",
    )(x2d, o2d, bf2d, gates, *ws)


def _moe_kernel(cnt_ref, start_ref, tok_hbm, w_hbm, hn_ref, wg_ref, wu_ref, wd_ref, out_ref,
                tok_s, w_s, *scratch, chunk, static_scatter):
    E = EXPERTS_PER_STEP
    g_bufs = (scratch[0:E], scratch[E:2 * E])
    y_bufs = (scratch[2 * E:3 * E], scratch[3 * E:4 * E])
    sem = scratch[4 * E]
    i = pl.program_id(0)
    p = pl.program_id(1)
    n_groups = N_EXPERTS // E
    R = ROWS_PER_TOKEN
    first = i * N_EXPERTS + p * E

    def tile_at(row):
        return pl.ds(pl.multiple_of(row * R, R), R)

    def token_tile(off):
        return pl.ds(pl.multiple_of(off, R), R)

    def gather_group(base, j0, gbuf):
        vals = [hn_ref[token_tile(tok_s[base + j0 + r]), :] for r in range(GROUP)]
        for r in range(GROUP):
            gbuf[tile_at(j0 + r), :] = vals[r]

    def scatter_group(base, j0, ybuf):
        dsts = [token_tile(tok_s[base + j0 + r]) for r in range(GROUP)]
        vals = [out_ref[dsts[r], :] + w_s[base + j0 + r] * ybuf[tile_at(j0 + r), :]
                for r in range(GROUP)]
        for r in range(GROUP):
            out_ref[dsts[r], :] = vals[r]

    def gather_rolled(base, rows, gbuf):
        def group(g, _):
            gather_group(base, g * GROUP, gbuf)
            return 0

        def row(j, _):
            gbuf[tile_at(j), :] = hn_ref[token_tile(tok_s[base + j]), :]
            return 0

        lax.fori_loop(0, rows // GROUP, group, 0)
        lax.fori_loop(rows // GROUP * GROUP, rows, row, 0)

    def scatter_rolled(base, first_row, rows, ybuf):
        def group(g, _):
            scatter_group(base, g * GROUP, ybuf)
            return 0

        def row(j, _):
            dst = token_tile(tok_s[base + j])
            out_ref[dst, :] = out_ref[dst, :] + w_s[base + j] * ybuf[tile_at(j), :]
            return 0

        lax.fori_loop(first_row // GROUP, rows // GROUP, group, 0)
        lax.fori_loop(jnp.maximum(rows // GROUP * GROUP, first_row), rows, row, 0)

    def ffn(k, gbuf, ybuf):
        xg = jnp.concatenate([gbuf[pl.ds(r, chunk, stride=R), :] for r in range(R)],
                             axis=1).astype(BF16)
        g = _dot(xg, wg_ref[k])
        u = _dot(xg, wu_ref[k])
        act = (jax.nn.silu(g) * u).astype(BF16)
        y = _dot(act, wd_ref[k])
        for r in range(R):
            ybuf[pl.ds(r, chunk, stride=R), :] = y[:, r * LANES:(r + 1) * LANES]

    ks = range(E)
    n = [cnt_ref[first + k] for k in ks]
    st = [start_ref[first + k] for k in ks]
    rows = [jnp.minimum(n[k], chunk) for k in ks]
    st_next = [start_ref[first + E + k] for k in ks]
    prev = [jnp.maximum(first - E + k, 0) for k in ks]
    st_prev = [start_ref[prev[k]] for k in ks]
    rows_prev = [jnp.where(p > 0, jnp.minimum(cnt_ref[prev[k]], chunk), 0) for k in ks]
    min_rows_prev = functools.reduce(jnp.minimum, rows_prev)

    @pl.when(jnp.logical_and(i == 0, p == 0))
    def _():
        for buf in g_bufs[0] + g_bufs[1]:
            buf[...] = jnp.zeros_like(buf)

    @pl.when(p == 0)
    def _():
        cp_t = pltpu.make_async_copy(tok_hbm.at[i], tok_s, sem.at[0])
        cp_w = pltpu.make_async_copy(w_hbm.at[i], w_s, sem.at[1])
        cp_t.start()
        cp_w.start()
        out_ref[...] = jnp.zeros_like(out_ref)
        cp_t.wait()
        cp_w.wait()
        for k in ks:
            gather_rolled(st[k], rows[k], g_bufs[0][k])

    def pipelined_step(parity):
        g_cur, g_next = g_bufs[parity], g_bufs[1 - parity]
        y_cur, y_prev = y_bufs[parity], y_bufs[1 - parity]

        @pl.when(min_rows_prev >= static_scatter)
        def _():
            for k in ks:
                for j0 in range(0, chunk, GROUP):
                    gather_group(st_next[k], j0, g_next[k])
            for k in ks:
                ffn(k, g_cur[k], y_cur[k])
            for k in ks:
                for j0 in range(0, static_scatter, GROUP):
                    scatter_group(st_prev[k], j0, y_prev[k])
            for k in ks:
                scatter_rolled(st_prev[k], static_scatter, rows_prev[k], y_prev[k])

        @pl.when(min_rows_prev < static_scatter)
        def _():
            for k in ks:
                gather_rolled(st_next[k], chunk, g_next[k])
                ffn(k, g_cur[k], y_cur[k])
                scatter_rolled(st_prev[k], 0, rows_prev[k], y_prev[k])

        @pl.when(p == n_groups - 1)
        def _():
            for k in ks:
                scatter_rolled(st[k], 0, rows[k], y_cur[k])

        for k in ks:
            def extra_chunk(c, _, k=k):
                base = st[k] + c * chunk
                rows_c = jnp.minimum(chunk, n[k] - c * chunk)
                gather_rolled(base, rows_c, g_cur[0])
                ffn(k, g_cur[0], y_prev[0])
                scatter_rolled(base, 0, rows_c, y_prev[0])
                return 0

            lax.fori_loop(1, pl.cdiv(n[k], chunk), extra_chunk, 0)

    @pl.when(p % 2 == 0)
    def _():
        pipelined_step(0)

    @pl.when(p % 2 == 1)
    def _():
        pipelined_step(1)


def _moe(hn_tiles, cnt, start, tok_sorted, w_sorted, wts, tmoe, chunk):
    R = ROWS_PER_TOKEN
    T = hn_tiles.shape[0] // R
    n_tiles = T // tmoe
    L = tok_sorted.shape[1]
    E = EXPERTS_PER_STEP
    wspec = lambda a: pl.BlockSpec((E,) + a.shape[1:], lambda i, p, *_: (p, 0, 0))
    tile_spec = pl.BlockSpec((tmoe * R, LANES), lambda i, p, *_: (i, 0))
    row_buf = pltpu.VMEM((chunk * R, LANES), F32)
    static_scatter = (chunk * 3 // 4) // GROUP * GROUP
    return pl.pallas_call(
        functools.partial(_moe_kernel, chunk=chunk, static_scatter=static_scatter),
        out_shape=jax.ShapeDtypeStruct((T * R, LANES), F32),
        grid_spec=pltpu.PrefetchScalarGridSpec(
            num_scalar_prefetch=2,
            grid=(n_tiles, N_EXPERTS // E),
            in_specs=[pl.BlockSpec(memory_space=pl.ANY),
                      pl.BlockSpec(memory_space=pl.ANY),
                      tile_spec,
                      wspec(wts['w_eg']), wspec(wts['w_eu']), wspec(wts['w_ed'])],
            out_specs=tile_spec,
            scratch_shapes=[pltpu.SMEM((L,), jnp.int32),
                            pltpu.SMEM((L,), F32)]
                           + [row_buf] * (4 * E)
                           + [pltpu.SemaphoreType.DMA((2,))]),
        compiler_params=pltpu.CompilerParams(dimension_semantics=("arbitrary", "arbitrary"),
                                             vmem_limit_bytes=VMEM_LIMIT),
        name="moe",
    )(cnt, start, tok_sorted, w_sorted, hn_tiles, wts['w_eg'], wts['w_eu'], wts['w_ed'])


def _dispatch(idx, w, tmoe, chunk):
    T = idx.shape[1]
    n_tiles = T // tmoe
    per_tile = lambda a: a.reshape(TOP_K, n_tiles, tmoe).transpose(1, 0, 2).reshape(n_tiles, TOP_K * tmoe)
    e_t, w_t = per_tile(idx), per_tile(w)
    row_off = jnp.arange(tmoe, dtype=jnp.int32) * ROWS_PER_TOKEN
    tok_l = jnp.broadcast_to(jnp.tile(row_off, TOP_K)[None, :], e_t.shape)
    _, tok_sorted, w_sorted = lax.sort((e_t, tok_l, w_t), dimension=1, num_keys=1)
    onehot = e_t[..., None] == jnp.arange(N_EXPERTS, dtype=jnp.int32)
    cnt = jnp.sum(onehot.astype(jnp.int32), axis=1)
    start = jnp.cumsum(cnt, axis=1) - cnt
    pad_list = lambda a: jnp.pad(a, ((0, 0), (0, chunk)))
    pad_step = lambda a: jnp.pad(a.reshape(-1), (0, EXPERTS_PER_STEP))
    return pad_step(cnt), pad_step(start), pad_list(tok_sorted), pad_list(w_sorted)


def _ple_kernel(base_ref, routed_ref, p_ref, gple_ref, wg_ref, wp_ref, y_ref):
    tm = base_ref.shape[0]
    routed = jnp.concatenate([routed_ref[pl.ds(r, tm, stride=ROWS_PER_TOKEN), :]
                              for r in range(ROWS_PER_TOKEN)], axis=1)
    x3 = base_ref[...] + routed
    hn = (x3 * _rms(x3, D_MODEL) * gple_ref[...]).astype(BF16)
    gate = jax.nn.sigmoid(_dot(hn, wg_ref[...]))
    y_ref[...] = x3 + gate * _dot(p_ref[...].astype(BF16), wp_ref[...])


def _ple(base, routed, p2d, wts, tm):
    T = base.shape[0]
    full = lambda a: pl.BlockSpec(a.shape, lambda i: (0,) * a.ndim)
    tok = lambda w: pl.BlockSpec((tm, w), lambda i: (i, 0))
    ws = [wts['g_ple'], wts['w_pg'], wts['w_pp']]
    return pl.pallas_call(
        _ple_kernel,
        out_shape=jax.ShapeDtypeStruct((T, D_MODEL), F32),
        grid=(T // tm,),
        in_specs=[tok(D_MODEL), pl.BlockSpec((tm * ROWS_PER_TOKEN, LANES), lambda i: (i, 0)),
                  tok(PLE_DIM)] + [full(w) for w in ws],
        out_specs=tok(D_MODEL),
        compiler_params=pltpu.CompilerParams(dimension_semantics=("arbitrary",),
                                             vmem_limit_bytes=VMEM_LIMIT),
        name="ple",
    )(base, routed, p2d, *ws)


def _head_pad_cols(w, width):
    K = w.shape[0]
    w = w.reshape(K, N_HEADS, width)
    return jnp.pad(w, ((0, 0), (0, 0), (0, HEAD_PAD - width))).reshape(K, N_HEADS * HEAD_PAD)


def _rot_partner(w):
    half = QK_ROPE // 2
    z = jnp.zeros_like(w[..., :QK_NOPE])
    return jnp.concatenate([z, -w[..., QK_NOPE + half:QK_HEAD], w[..., QK_NOPE:QK_NOPE + half]], axis=-1)


def _prep_weights(g_mix, w_in, b_gate, g_qlat, w_q_up, g_kvlat, w_kv_up, g_qn, g_kn, w_a, w_f,
                  w_o, g_ffn, w_router, b_router, w_e_gate, w_e_up, w_e_down, w_s_gate, w_s_up,
                  w_s_down, g_ple, w_ple_gate, w_ple_proj):
    half = QK_ROPE // 2
    off_kv, off_kr = Q_LORA, Q_LORA + KV_LORA
    off_f = off_kr + QK_ROPE
    off_g = off_f + FNET_W
    w1 = jnp.concatenate([
        w_in[:, :off_kr],
        jnp.pad(w_in[:, off_kr:off_f], ((0, 0), (0, LANES - QK_ROPE))),
        w_in[:, off_f:]], axis=1).astype(BF16)
    wq3 = w_q_up.reshape(Q_LORA, N_HEADS, QK_HEAD)
    wq = _head_pad_cols(w_q_up, QK_HEAD).astype(BF16)
    wqr = _head_pad_cols(_rot_partner(wq3).reshape(Q_LORA, -1), QK_HEAD).astype(BF16)
    wkv3 = w_kv_up.reshape(KV_LORA, N_HEADS, QK_NOPE + V_HEAD)
    wk = _head_pad_cols(wkv3[:, :, :QK_NOPE].reshape(KV_LORA, -1), QK_NOPE).astype(BF16)
    wv = _head_pad_cols(wkv3[:, :, QK_NOPE:].reshape(KV_LORA, -1), V_HEAD).astype(BF16)
    vone = jnp.tile(jnp.concatenate([jnp.zeros((V_HEAD,), F32), jnp.ones((HEAD_PAD - V_HEAD,), F32)]),
                    N_HEADS).reshape(1, N_HEADS * HEAD_PAD)
    eye = jnp.eye(QK_ROPE, dtype=F32)
    place = jnp.concatenate([jnp.zeros((QK_ROPE, QK_NOPE), F32), eye], axis=1)
    place_rot = _rot_partner(place)
    ekr = jnp.concatenate([
        _head_pad_cols(jnp.tile(place, (1, N_HEADS)), QK_HEAD),
        _head_pad_cols(jnp.tile(place_rot, (1, N_HEADS)), QK_HEAD)], axis=1)
    ekr = jnp.pad(ekr, ((0, LANES - QK_ROPE), (0, 0))).astype(BF16)

    def gain_tabs(g):
        gp = jnp.concatenate([jnp.zeros((QK_NOPE,), F32), g[QK_NOPE + half:], g[QK_NOPE:QK_NOPE + half]])
        pad = lambda a: jnp.pad(a, (0, HEAD_PAD - QK_HEAD)).reshape(1, HEAD_PAD)
        return pad(g), pad(gp)

    gqc, gqs = gain_tabs(g_qn)
    gkc, gks = gain_tabs(g_kn)
    j = jnp.arange(FNET_GROUP_W, dtype=jnp.int32)
    ang = (2.0 * np.pi / FNET_GROUP_W) * ((j[:, None] * j[None, :]) % FNET_GROUP_W).astype(F32)
    eye_g = jnp.eye(FNET_GROUPS, dtype=F32)
    bd = jnp.concatenate([jnp.kron(eye_g, jnp.cos(ang)), jnp.kron(eye_g, jnp.sin(ang))], axis=1).astype(BF16)
    row = lambda a: a.reshape(1, -1).astype(F32)
    return {
        'g_mix': row(g_mix), 'w1': w1, 'b_gate': row(b_gate), 'g_qlat': row(g_qlat),
        'g_kvlat': row(g_kvlat), 'wq': wq, 'wqr': wqr, 'wk': wk, 'wv': wv, 'vone': vone, 'ekr': ekr,
        'gqc': gqc, 'gqs': gqs, 'gkc': gkc, 'gks': gks, 'bd': bd,
        'w_a': w_a.astype(BF16), 'w_f': w_f.astype(BF16), 'w_o': w_o.astype(BF16),
        'g_ffn': row(g_ffn),
        'w_sgu': jnp.concatenate([w_s_gate, w_s_up], axis=1).astype(BF16),
        'w_sd': w_s_down.astype(BF16),
        'w_rt': w_router.T.astype(F32), 'b_r': b_router.reshape(N_EXPERTS, 1).astype(F32),
        'w_eg': w_e_gate.astype(BF16), 'w_eu': w_e_up.astype(BF16), 'w_ed': w_e_down.astype(BF16),
        'g_ple': row(g_ple), 'w_pg': w_ple_gate.astype(BF16), 'w_pp': w_ple_proj.astype(BF16),
    }


def _tables(S):
    half = QK_ROPE // 2
    freqs = 1.0 / (ROPE_THETA ** (jnp.arange(half, dtype=F32) / half))
    ang = jnp.arange(S, dtype=F32)[:, None] * freqs[None, :]
    cos, sin = jnp.cos(ang), jnp.sin(ang)
    cos_t = jnp.concatenate([jnp.ones((S, QK_NOPE), F32), cos, cos,
                             jnp.zeros((S, HEAD_PAD - QK_HEAD), F32)], axis=1)
    sin_t = jnp.concatenate([jnp.zeros((S, QK_NOPE), F32), sin, sin,
                             jnp.zeros((S, HEAD_PAD - QK_HEAD), F32)], axis=1)
    split = 64
    j = jnp.arange(S, dtype=jnp.int32)[:, None]
    ka = jnp.arange(S // split, dtype=jnp.int32)[None, :] * split
    kb = jnp.arange(split, dtype=jnp.int32)[None, :]
    ang_a = (2.0 * np.pi / S) * ((j * ka) % S).astype(F32)
    ang_b = (2.0 * np.pi / S) * ((j * kb) % S).astype(F32)
    ca, sa = jnp.cos(ang_a)[:, :, None], jnp.sin(ang_a)[:, :, None]
    cb, sb = jnp.cos(ang_b)[:, None, :], jnp.sin(ang_b)[:, None, :]
    norm = (S * FNET_GROUP_W) ** -0.5
    dft_c = ((ca * cb - sa * sb) * norm).reshape(S, S).astype(BF16)
    dft_s = ((sa * cb + ca * sb) * -norm).reshape(S, S).astype(BF16)
    return {'cos': cos_t, 'sin': sin_t, 'dft_c': dft_c, 'dft_s': dft_s}


def _block(x, p, wts, tabs, *, tm, tq, tk, tdft, tmoe, chunk):
    B, S, D = x.shape
    T = B * S
    x2d = x.reshape(T, D)
    q, k, v, fcs, gates = _inproj(x2d, S, tabs, wts, tm)
    o = _attention(q.reshape(B, S, -1), k.reshape(B, S, -1), v.reshape(B, S, -1), tq, tk)
    bf = _seqdft(fcs.reshape(B, S, -1), tabs['dft_c'], tabs['dft_s'], tdft)
    hn, base, idx, w = _merge(x2d, o.reshape(T, -1), bf.reshape(T, -1), gates, wts, tm)
    tmoe = min(tmoe, T)
    cnt, start, tok_sorted, w_sorted = _dispatch(idx, w, tmoe, chunk)
    routed = _moe(hn, cnt, start, tok_sorted, w_sorted, wts, tmoe, chunk)
    y = _ple(base, routed, p.reshape(T, -1), wts, tm)
    return y.reshape(B, S, D)


def kernel(x_prompt, x_sample, p_prompt, p_sample, g_mix, w_in, b_gate, g_qlat, w_q_up, g_kvlat, w_kv_up, g_qn, g_kn, w_a, w_f, w_o, g_ffn, w_router, b_router, w_e_gate, w_e_up, w_e_down, w_s_gate, w_s_up, w_s_down, g_ple, w_ple_gate, w_ple_proj):
    params = (g_mix, w_in, b_gate, g_qlat, w_q_up, g_kvlat, w_kv_up, g_qn, g_kn, w_a, w_f, w_o,
              g_ffn, w_router, b_router, w_e_gate, w_e_up, w_e_down, w_s_gate, w_s_up, w_s_down,
              g_ple, w_ple_gate, w_ple_proj)
    depth = g_mix.shape[0]
    cfg = dict(tm=512, tq=512, tk=1024, tdft=512, tmoe=2048, chunk=288)
    xp, xs = x_prompt, x_sample
    tabs = _tables(x_prompt.shape[1])
    tabs_s = tabs if x_sample.shape[1] == x_prompt.shape[1] else _tables(x_sample.shape[1])
    for l in range(depth):
        wts = _prep_weights(*[a[l] for a in params])
        xp = _block(xp, p_prompt[l], wts, tabs, **cfg)
        xs = _block(xs, p_sample[l], wts, tabs_s, **cfg)
    return (xp, xs)
```

```python
import functools

import jax
import jax.numpy as jnp
import numpy as np
from jax import lax
from jax.experimental import pallas as pl
from jax.experimental.pallas import tpu as pltpu

D_MODEL = 1024
N_HEADS = 8
QK_NOPE = 64
QK_ROPE = 32
QK_HEAD = QK_NOPE + QK_ROPE
V_HEAD = 64
Q_LORA = 384
KV_LORA = 256
ROPE_THETA = 10000.0
FNET_GROUPS = 4
FNET_GROUP_W = 128
FNET_W = FNET_GROUPS * FNET_GROUP_W
N_EXPERTS = 64
TOP_K = 8
N_EXPERT_GROUPS = 8
TOPK_GROUPS = 4
EXPERT_FF = 256
SHARED_FF = 256
ROUTED_SCALE = 2.5
PLE_DIM = 256
EPS = 1e-6

LANES = 128
SUBLANES = 8
HEAD_PAD = LANES
ROWS_PER_TOKEN = D_MODEL // LANES
GROUP = 8
EXPERTS_PER_STEP = 2
HEADS_PER_STEP = 2

C_Q = 0
C_KV = C_Q + Q_LORA
C_KR = C_KV + KV_LORA
C_F = C_KR + LANES
C_G = C_F + FNET_W
C_END = C_G + 2 * D_MODEL

VMEM_LIMIT = 56 * 1024 * 1024
BF16 = jnp.bfloat16
F32 = jnp.float32


def _rms(x, n):
    return lax.rsqrt(jnp.sum(x * x, axis=-1, keepdims=True) * (1.0 / n) + EPS)


def _dot(a, b):
    return jnp.dot(a, b, preferred_element_type=F32)


def _inproj_kernel(x_ref, cos_ref, sin_ref, gmix_ref, w1_ref, bg_ref, gq_ref, gkv_ref,
                   wq_ref, wqr_ref, wk_ref, wv_ref, vone_ref, ekr_ref, gqc_ref, gqs_ref, gkc_ref,
                   gks_ref, bd_ref, q_out, k_out, v_out, f_out, g_out):
    x = x_ref[...]
    h = (x * _rms(x, D_MODEL) * gmix_ref[...]).astype(BF16)
    proj = _dot(h, w1_ref[...])
    q_lat = proj[:, C_Q:C_KV]
    q_lat = (q_lat * _rms(q_lat, Q_LORA) * gq_ref[...]).astype(BF16)
    kv_lat = proj[:, C_KV:C_KR]
    kv_lat = (kv_lat * _rms(kv_lat, KV_LORA) * gkv_ref[...]).astype(BF16)
    kr = proj[:, C_KR:C_F].astype(BF16)
    f_in = proj[:, C_F:C_G].astype(BF16)
    g_out[...] = jax.nn.sigmoid(proj[:, C_G:C_END] + bg_ref[...]).astype(BF16)
    f_out[...] = _dot(f_in, bd_ref[...]).astype(BF16)
    v_out[...] = (_dot(kv_lat, wv_ref[...]) + vone_ref[...]).astype(BF16)

    q = _dot(q_lat, wq_ref[...])
    qr = _dot(q_lat, wqr_ref[...])
    kre = _dot(kr, ekr_ref[...])
    k = _dot(kv_lat, wk_ref[...]) + kre[:, :N_HEADS * HEAD_PAD]
    kr_rot = kre[:, N_HEADS * HEAD_PAD:]
    cos = cos_ref[...]
    sin = sin_ref[...]
    qc = gqc_ref[...] * cos
    qs = gqs_ref[...] * sin
    kc = gkc_ref[...] * cos
    ks = gks_ref[...] * sin
    scale = QK_HEAD ** -0.5 * np.log2(np.e)
    for hd in range(N_HEADS):
        sl = slice(hd * HEAD_PAD, (hd + 1) * HEAD_PAD)
        qh = q[:, sl]
        rq = _rms(qh, QK_HEAD) * scale
        q_out[:, sl] = ((qh * qc + qr[:, sl] * qs) * rq).astype(BF16)
        kh = k[:, sl]
        rk = _rms(kh, QK_HEAD)
        k_out[:, sl] = ((kh * kc + kr_rot[:, sl] * ks) * rk).astype(BF16)


def _inproj(x2d, S, tabs, wts, tm):
    T = x2d.shape[0]
    n_pos = S // tm
    full = lambda a: pl.BlockSpec(a.shape, lambda i: (0,) * a.ndim)
    tok = lambda w: pl.BlockSpec((tm, w), lambda i: (i, 0))
    pos = pl.BlockSpec((tm, HEAD_PAD), lambda i: (i % n_pos, 0))
    names = ['g_mix', 'w1', 'b_gate', 'g_qlat', 'g_kvlat', 'wq', 'wqr', 'wk', 'wv', 'vone', 'ekr',
             'gqc', 'gqs', 'gkc', 'gks', 'bd']
    ws = [wts[n] for n in names]
    return pl.pallas_call(
        _inproj_kernel,
        out_shape=(jax.ShapeDtypeStruct((T, N_HEADS * HEAD_PAD), BF16),
                   jax.ShapeDtypeStruct((T, N_HEADS * HEAD_PAD), BF16),
                   jax.ShapeDtypeStruct((T, N_HEADS * HEAD_PAD), BF16),
                   jax.ShapeDtypeStruct((T, 2 * FNET_W), BF16),
                   jax.ShapeDtypeStruct((T, 2 * D_MODEL), BF16)),
        grid=(T // tm,),
        in_specs=[tok(D_MODEL), pos, pos] + [full(w) for w in ws],
        out_specs=(tok(N_HEADS * HEAD_PAD), tok(N_HEADS * HEAD_PAD), tok(N_HEADS * HEAD_PAD),
                   tok(2 * FNET_W), tok(2 * D_MODEL)),
        compiler_params=pltpu.CompilerParams(dimension_semantics=("arbitrary",),
                                             vmem_limit_bytes=VMEM_LIMIT),
        name="inproj",
    )(x2d, tabs['cos'], tabs['sin'], *ws)


def _attn_kernel(q_ref, k_ref, v_ref, o_ref, s_a, s_b, *, tk):
    S = k_ref.shape[0]
    tq = q_ref.shape[0]
    nblk = tk // LANES
    n_chunks = S // tk
    heads = [slice(a * HEAD_PAD, (a + 1) * HEAD_PAD) for a in range(HEADS_PER_STEP)]
    s_bufs = (s_a, s_b)

    def scores(c, s_ref):
        for a, hs in enumerate(heads):
            s_ref[a] = lax.dot_general(q_ref[:, hs], k_ref[c * tk:(c + 1) * tk, hs],
                                       (((1,), (1,)), ((), ())), preferred_element_type=F32)

    def consume(c, s_ref, carry):
        new = []
        for a, hs in enumerate(heads):
            m, acc = carry[a]
            mx = s_ref[a, :, :LANES]
            for j in range(1, nblk):
                mx = jnp.maximum(mx, s_ref[a, :, j * LANES:(j + 1) * LANES])
            m_new = jnp.maximum(m, jnp.max(mx, axis=-1, keepdims=True))
            alpha = jnp.exp2(m - m_new)
            p = jnp.concatenate(
                [jnp.exp2((s_ref[a, :, j * LANES:(j + 1) * LANES] - m_new).astype(BF16))
                 for j in range(nblk)], axis=1)
            acc = alpha * acc + _dot(p, v_ref[c * tk:(c + 1) * tk, hs])
            new.append((m_new, acc))
        return new

    carry = [(jnp.full((tq, LANES), -jnp.inf, F32), jnp.zeros((tq, HEAD_PAD), F32))
             for _ in heads]
    scores(0, s_bufs[0])
    for c in range(n_chunks):
        if c + 1 < n_chunks:
            scores(c + 1, s_bufs[(c + 1) % 2])
        carry = consume(c, s_bufs[c % 2], carry)
    lane = lax.broadcasted_iota(jnp.int32, (tq, 2 * V_HEAD), 1)
    for pair in range(HEADS_PER_STEP // 2):
        acc_a, acc_b = carry[2 * pair][1], carry[2 * pair + 1][1]
        o_a = acc_a / pltpu.roll(acc_a, V_HEAD, 1)
        o_b = acc_b / pltpu.roll(acc_b, V_HEAD, 1)
        o_ref[:, pair * 2 * V_HEAD:(pair + 1) * 2 * V_HEAD] = jnp.where(
            lane < V_HEAD, o_a, pltpu.roll(o_b, V_HEAD, 1)).astype(o_ref.dtype)


def _attention(q, k, v, tq, tk):
    B, S, _ = q.shape
    hp = HEADS_PER_STEP
    return pl.pallas_call(
        functools.partial(_attn_kernel, tk=tk),
        out_shape=jax.ShapeDtypeStruct((B, S, N_HEADS * V_HEAD), BF16),
        grid=(B, N_HEADS // hp, S // tq),
        in_specs=[pl.BlockSpec((None, tq, hp * HEAD_PAD), lambda b, p, i: (b, i, p)),
                  pl.BlockSpec((None, S, hp * HEAD_PAD), lambda b, p, i: (b, 0, p)),
                  pl.BlockSpec((None, S, hp * HEAD_PAD), lambda b, p, i: (b, 0, p))],
        out_specs=pl.BlockSpec((None, tq, hp * V_HEAD), lambda b, p, i: (b, i, p)),
        scratch_shapes=[pltpu.VMEM((hp, tq, tk), F32), pltpu.VMEM((hp, tq, tk), F32)],
        compiler_params=pltpu.CompilerParams(
            dimension_semantics=("arbitrary", "arbitrary", "arbitrary"),
            vmem_limit_bytes=VMEM_LIMIT),
        name="attention",
    )(q, k, v)


def _seqdft_kernel(c_ref, s_ref, f_ref, o_ref):
    y = _dot(c_ref[...], f_ref[:, :FNET_W]) + _dot(s_ref[...], f_ref[:, FNET_W:])
    o_ref[...] = y.astype(o_ref.dtype)


def _seqdft(fcs, ctab, stab, tm):
    B, S, _ = fcs.shape
    return pl.pallas_call(
        _seqdft_kernel,
        out_shape=jax.ShapeDtypeStruct((B, S, FNET_W), BF16),
        grid=(S // tm, B),
        in_specs=[pl.BlockSpec((tm, S), lambda i, b: (i, 0)),
                  pl.BlockSpec((tm, S), lambda i, b: (i, 0)),
                  pl.BlockSpec((None, S, 2 * FNET_W), lambda i, b: (b, 0, 0))],
        out_specs=pl.BlockSpec((None, tm, FNET_W), lambda i, b: (b, i, 0)),
        compiler_params=pltpu.CompilerParams(dimension_semantics=("arbitrary", "arbitrary"),
                                             vmem_limit_bytes=VMEM_LIMIT),
        name="seqdft",
    )(ctab, stab, fcs)


def _first_argmax(v, iota, n):
    m = jnp.max(v, axis=0, keepdims=True)
    i = jnp.min(jnp.where(v == m, iota, n), axis=0, keepdims=True)
    return m, i


def _route(logits_t, bias):
    tm = logits_t.shape[1]
    per = N_EXPERTS // N_EXPERT_GROUPS
    scores = jax.nn.sigmoid(logits_t)
    sel = scores + bias
    neg = jnp.float32(-jnp.inf)
    iota_p = lax.broadcasted_iota(jnp.int32, (per, tm), 0)
    iota_g = lax.broadcasted_iota(jnp.int32, (N_EXPERT_GROUPS, tm), 0)
    gscore = jnp.zeros((N_EXPERT_GROUPS, tm), F32)
    for g in range(N_EXPERT_GROUPS):
        v = sel[g * per:(g + 1) * per, :]
        m1, i1 = _first_argmax(v, iota_p, per)
        m2 = jnp.max(jnp.where(iota_p == i1, neg, v), axis=0, keepdims=True)
        gscore = jnp.where(iota_g == g, m1 + m2, gscore)
    gmask = jnp.zeros((N_EXPERT_GROUPS, tm), jnp.bool_)
    v = gscore
    for _ in range(TOPK_GROUPS):
        _, i = _first_argmax(v, iota_g, N_EXPERT_GROUPS)
        hit = iota_g == i
        gmask = jnp.logical_or(gmask, hit)
        v = jnp.where(hit, neg, v)
    gkeep = jnp.where(gmask, 1.0, 0.0)
    parts = []
    for g in range(N_EXPERT_GROUPS):
        keep = gkeep[g:g + 1, :] > 0.5
        parts.append(jnp.where(keep, sel[g * per:(g + 1) * per, :], neg))
    v = jnp.concatenate(parts, axis=0)
    iota_e = lax.broadcasted_iota(jnp.int32, (N_EXPERTS, tm), 0)
    iota_k = lax.broadcasted_iota(jnp.int32, (TOP_K, tm), 0)
    idx = jnp.zeros((TOP_K, tm), jnp.int32)
    wts = jnp.zeros((TOP_K, tm), F32)
    for kk in range(TOP_K):
        _, i = _first_argmax(v, iota_e, N_EXPERTS)
        hit = iota_e == i
        wk = jnp.sum(jnp.where(hit, scores, 0.0), axis=0, keepdims=True)
        v = jnp.where(hit, neg, v)
        idx = jnp.where(iota_k == kk, i, idx)
        wts = jnp.where(iota_k == kk, wk, wts)
    wts = wts / jnp.sum(wts, axis=0, keepdims=True) * ROUTED_SCALE
    return idx, wts


def _merge_kernel(x_ref, o_ref, bf_ref, g_ref, wa_ref, wf_ref, wo_ref, gffn_ref, wsgu_ref,
                  wsd_ref, wrt_ref, br_ref, hn_out, base_out, idx_out, w_out):
    a = _dot(o_ref[...], wa_ref[...])
    b = _dot(bf_ref[...], wf_ref[...])
    g = g_ref[...].astype(F32)
    m = (g[:, :D_MODEL] * a + g[:, D_MODEL:] * b).astype(BF16)
    x2 = x_ref[...] + _dot(m, wo_ref[...])
    hn = x2 * _rms(x2, D_MODEL) * gffn_ref[...]
    tm = hn.shape[0]
    for r in range(ROWS_PER_TOKEN):
        hn_out[pl.ds(r, tm, stride=ROWS_PER_TOKEN), :] = hn[:, r * LANES:(r + 1) * LANES]
    hb = hn.astype(BF16)
    gu = _dot(hb, wsgu_ref[...])
    act = (jax.nn.silu(gu[:, :SHARED_FF]) * gu[:, SHARED_FF:]).astype(BF16)
    base_out[...] = x2 + _dot(act, wsd_ref[...])
    logits_t = lax.dot_general(wrt_ref[...], hn, (((1,), (1,)), ((), ())),
                               precision=lax.Precision.HIGHEST, preferred_element_type=F32)
    idx, wts = _route(logits_t, br_ref[...])
    idx_out[...] = idx
    w_out[...] = wts


def _merge(x2d, o2d, bf2d, gates, wts, tm):
    T = x2d.shape[0]
    full = lambda a: pl.BlockSpec(a.shape, lambda i: (0,) * a.ndim)
    tok = lambda w: pl.BlockSpec((tm, w), lambda i: (i, 0))
    names = ['w_a', 'w_f', 'w_o', 'g_ffn', 'w_sgu', 'w_sd', 'w_rt', 'b_r']
    ws = [wts[n] for n in names]
    return pl.pallas_call(
        _merge_kernel,
        out_shape=(jax.ShapeDtypeStruct((T * ROWS_PER_TOKEN, LANES), F32),
                   jax.ShapeDtypeStruct((T, D_MODEL), F32),
                   jax.ShapeDtypeStruct((TOP_K, T), jnp.int32),
                   jax.ShapeDtypeStruct((TOP_K, T), F32)),
        grid=(T // tm,),
        in_specs=[tok(D_MODEL), tok(N_HEADS * V_HEAD), tok(FNET_W), tok(2 * D_MODEL)]
                 + [full(w) for w in ws],
        out_specs=(pl.BlockSpec((tm * ROWS_PER_TOKEN, LANES), lambda i: (i, 0)), tok(D_MODEL),
                   pl.BlockSpec((TOP_K, tm), lambda i: (0, i)),
                   pl.BlockSpec((TOP_K, tm), lambda i: (0, i))),
        compiler_params=pltpu.CompilerParams(dimension_semantics=("arbitrary",),
                                             vmem_limit_bytes=VMEM_LIMIT),
        name="merge",
    )(x2d, o2d, bf2d, gates, *ws)


def _moe_kernel(cnt_ref, start_ref, tok_hbm, w_hbm, hn_ref, wg_ref, wu_ref, wd_ref, out_ref,
                tok_s, w_s, *scratch, chunk, static_scatter):
    E = EXPERTS_PER_STEP
    g_bufs = scratch[0:E]
    x_bufs = (scratch[E:2 * E], scratch[2 * E:3 * E])
    y_bufs = (scratch[3 * E:4 * E], scratch[4 * E:5 * E])
    sem = scratch[5 * E]
    i = pl.program_id(0)
    p = pl.program_id(1)
    n_groups = N_EXPERTS // E
    R = ROWS_PER_TOKEN
    first = i * N_EXPERTS + p * E

    def tile_at(row):
        return pl.ds(pl.multiple_of(row * R, R), R)

    def token_tile(off):
        return pl.ds(pl.multiple_of(off, R), R)

    def gather_group(base, j0, gbuf):
        vals = [hn_ref[token_tile(tok_s[base + j0 + r]), :] for r in range(GROUP)]
        for r in range(GROUP):
            gbuf[tile_at(j0 + r), :] = vals[r]

    def scatter_group(base, j0, ybuf):
        dsts = [token_tile(tok_s[base + j0 + r]) for r in range(GROUP)]
        vals = [out_ref[dsts[r], :] + w_s[base + j0 + r] * ybuf[tile_at(j0 + r), :]
                for r in range(GROUP)]
        for r in range(GROUP):
            out_ref[dsts[r], :] = vals[r]

    def gather_rolled(base, rows, gbuf):
        def group(g, _):
            gather_group(base, g * GROUP, gbuf)
            return 0

        def row(j, _):
            gbuf[tile_at(j), :] = hn_ref[token_tile(tok_s[base + j]), :]
            return 0

        lax.fori_loop(0, rows // GROUP, group, 0)
        lax.fori_loop(rows // GROUP * GROUP, rows, row, 0)

    def scatter_rolled(base, first_row, rows, ybuf):
        def group(g, _):
            scatter_group(base, g * GROUP, ybuf)
            return 0

        def row(j, _):
            dst = token_tile(tok_s[base + j])
            out_ref[dst, :] = out_ref[dst, :] + w_s[base + j] * ybuf[tile_at(j), :]
            return 0

        lax.fori_loop(first_row // GROUP, rows // GROUP, group, 0)
        lax.fori_loop(jnp.maximum(rows // GROUP * GROUP, first_row), rows, row, 0)

    def to_matmul_layout(gbuf, xbuf):
        xbuf[...] = jnp.concatenate([gbuf[pl.ds(r, chunk, stride=R), :] for r in range(R)],
                                    axis=1).astype(BF16)

    def ffn_hidden(k, xbuf):
        xg = xbuf[...]
        g = _dot(xg, wg_ref[k])
        u = _dot(xg, wu_ref[k])
        return (jax.nn.silu(g) * u).astype(BF16)

    def ffn_out(k, act, ybuf):
        y = _dot(act, wd_ref[k])
        for r in range(R):
            ybuf[pl.ds(r, chunk, stride=R), :] = y[:, r * LANES:(r + 1) * LANES]

    ks = range(E)
    n = [cnt_ref[first + k] for k in ks]
    st = [start_ref[first + k] for k in ks]
    rows = [jnp.minimum(n[k], chunk) for k in ks]
    st_next = [start_ref[first + E + k] for k in ks]
    prev = [jnp.maximum(first - E + k, 0) for k in ks]
    st_prev = [start_ref[prev[k]] for k in ks]
    rows_prev = [jnp.where(p > 0, jnp.minimum(cnt_ref[prev[k]], chunk), 0) for k in ks]
    min_rows_prev = functools.reduce(jnp.minimum, rows_prev)

    @pl.when(jnp.logical_and(i == 0, p == 0))
    def _():
        for buf in g_bufs:
            buf[...] = jnp.zeros_like(buf)

    @pl.when(p == 0)
    def _():
        cp_t = pltpu.make_async_copy(tok_hbm.at[i], tok_s, sem.at[0])
        cp_w = pltpu.make_async_copy(w_hbm.at[i], w_s, sem.at[1])
        cp_t.start()
        cp_w.start()
        out_ref[...] = jnp.zeros_like(out_ref)
        cp_t.wait()
        cp_w.wait()
        for k in ks:
            gather_rolled(st[k], rows[k], g_bufs[k])
            to_matmul_layout(g_bufs[k], x_bufs[0][k])

    def pipelined_step(parity):
        x_cur, x_next = x_bufs[parity], x_bufs[1 - parity]
        y_cur, y_prev = y_bufs[parity], y_bufs[1 - parity]

        @pl.when(min_rows_prev >= static_scatter)
        def _():
            for k in ks:
                for j0 in range(0, chunk, GROUP):
                    gather_group(st_next[k], j0, g_bufs[k])
                to_matmul_layout(g_bufs[k], x_next[k])
            acts = [ffn_hidden(k, x_cur[k]) for k in ks]
            for k in ks:
                ffn_out(k, acts[k], y_cur[k])
            for k in ks:
                for j0 in range(0, static_scatter, GROUP):
                    scatter_group(st_prev[k], j0, y_prev[k])
            for k in ks:
                scatter_rolled(st_prev[k], static_scatter, rows_prev[k], y_prev[k])

        @pl.when(min_rows_prev < static_scatter)
        def _():
            for k in ks:
                gather_rolled(st_next[k], chunk, g_bufs[k])
                to_matmul_layout(g_bufs[k], x_next[k])
                ffn_out(k, ffn_hidden(k, x_cur[k]), y_cur[k])
                scatter_rolled(st_prev[k], 0, rows_prev[k], y_prev[k])

        @pl.when(p == n_groups - 1)
        def _():
            for k in ks:
                scatter_rolled(st[k], 0, rows[k], y_cur[k])

        for k in ks:
            def extra_chunk(c, _, k=k):
                base = st[k] + c * chunk
                rows_c = jnp.minimum(chunk, n[k] - c * chunk)
                gather_rolled(base, rows_c, g_bufs[0])
                to_matmul_layout(g_bufs[0], x_cur[0])
                ffn_out(k, ffn_hidden(k, x_cur[0]), y_prev[0])
                scatter_rolled(base, 0, rows_c, y_prev[0])
                return 0

            lax.fori_loop(1, pl.cdiv(n[k], chunk), extra_chunk, 0)

    @pl.when(p % 2 == 0)
    def _():
        pipelined_step(0)

    @pl.when(p % 2 == 1)
    def _():
        pipelined_step(1)


def _moe(hn_tiles, cnt, start, tok_sorted, w_sorted, wts, tmoe, chunk):
    R = ROWS_PER_TOKEN
    T = hn_tiles.shape[0] // R
    n_tiles = T // tmoe
    L = tok_sorted.shape[1]
    E = EXPERTS_PER_STEP
    wspec = lambda a: pl.BlockSpec((E,) + a.shape[1:], lambda i, p, *_: (p, 0, 0))
    tile_spec = pl.BlockSpec((tmoe * R, LANES), lambda i, p, *_: (i, 0))
    row_buf = pltpu.VMEM((chunk * R, LANES), F32)
    static_scatter = (chunk * 3 // 4) // GROUP * GROUP
    return pl.pallas_call(
        functools.partial(_moe_kernel, chunk=chunk, static_scatter=static_scatter),
        out_shape=jax.ShapeDtypeStruct((T * R, LANES), F32),
        grid_spec=pltpu.PrefetchScalarGridSpec(
            num_scalar_prefetch=2,
            grid=(n_tiles, N_EXPERTS // E),
            in_specs=[pl.BlockSpec(memory_space=pl.ANY),
                      pl.BlockSpec(memory_space=pl.ANY),
                      tile_spec,
                      wspec(wts['w_eg']), wspec(wts['w_eu']), wspec(wts['w_ed'])],
            out_specs=tile_spec,
            scratch_shapes=[pltpu.SMEM((L,), jnp.int32),
                            pltpu.SMEM((L,), F32)]
                           + [row_buf] * E
                           + [pltpu.VMEM((chunk, D_MODEL), BF16)] * (2 * E)
                           + [row_buf] * (2 * E)
                           + [pltpu.SemaphoreType.DMA((2,))]),
        compiler_params=pltpu.CompilerParams(dimension_semantics=("arbitrary", "arbitrary"),
                                             vmem_limit_bytes=VMEM_LIMIT),
        name="moe",
    )(cnt, start, tok_sorted, w_sorted, hn_tiles, wts['w_eg'], wts['w_eu'], wts['w_ed'])


def _dispatch(idx, w, tmoe, chunk):
    T = idx.shape[1]
    n_tiles = T // tmoe
    per_tile = lambda a: a.reshape(TOP_K, n_tiles, tmoe).transpose(1, 0, 2).reshape(n_tiles, TOP_K * tmoe)
    e_t, w_t = per_tile(idx), per_tile(w)
    row_off = jnp.arange(tmoe, dtype=jnp.int32) * ROWS_PER_TOKEN
    tok_l = jnp.broadcast_to(jnp.tile(row_off, TOP_K)[None, :], e_t.shape)
    _, tok_sorted, w_sorted = lax.sort((e_t, tok_l, w_t), dimension=1, num_keys=1)
    onehot = e_t[..., None] == jnp.arange(N_EXPERTS, dtype=jnp.int32)
    cnt = jnp.sum(onehot.astype(jnp.int32), axis=1)
    start = jnp.cumsum(cnt, axis=1) - cnt
    pad_list = lambda a: jnp.pad(a, ((0, 0), (0, chunk)))
    pad_step = lambda a: jnp.pad(a.reshape(-1), (0, EXPERTS_PER_STEP))
    return pad_step(cnt), pad_step(start), pad_list(tok_sorted), pad_list(w_sorted)


def _ple_kernel(base_ref, routed_ref, p_ref, gple_ref, wg_ref, wp_ref, y_ref):
    tm = base_ref.shape[0]
    routed = jnp.concatenate([routed_ref[pl.ds(r, tm, stride=ROWS_PER_TOKEN), :]
                              for r in range(ROWS_PER_TOKEN)], axis=1)
    x3 = base_ref[...] + routed
    hn = (x3 * _rms(x3, D_MODEL) * gple_ref[...]).astype(BF16)
    gate = jax.nn.sigmoid(_dot(hn, wg_ref[...]))
    y_ref[...] = x3 + gate * _dot(p_ref[...].astype(BF16), wp_ref[...])


def _ple(base, routed, p2d, wts, tm):
    T = base.shape[0]
    full = lambda a: pl.BlockSpec(a.shape, lambda i: (0,) * a.ndim)
    tok = lambda w: pl.BlockSpec((tm, w), lambda i: (i, 0))
    ws = [wts['g_ple'], wts['w_pg'], wts['w_pp']]
    return pl.pallas_call(
        _ple_kernel,
        out_shape=jax.ShapeDtypeStruct((T, D_MODEL), F32),
        grid=(T // tm,),
        in_specs=[tok(D_MODEL), pl.BlockSpec((tm * ROWS_PER_TOKEN, LANES), lambda i: (i, 0)),
                  tok(PLE_DIM)] + [full(w) for w in ws],
        out_specs=tok(D_MODEL),
        compiler_params=pltpu.CompilerParams(dimension_semantics=("arbitrary",),
                                             vmem_limit_bytes=VMEM_LIMIT),
        name="ple",
    )(base, routed, p2d, *ws)


def _head_pad_cols(w, width):
    K = w.shape[0]
    w = w.reshape(K, N_HEADS, width)
    return jnp.pad(w, ((0, 0), (0, 0), (0, HEAD_PAD - width))).reshape(K, N_HEADS * HEAD_PAD)


def _rot_partner(w):
    half = QK_ROPE // 2
    z = jnp.zeros_like(w[..., :QK_NOPE])
    return jnp.concatenate([z, -w[..., QK_NOPE + half:QK_HEAD], w[..., QK_NOPE:QK_NOPE + half]], axis=-1)


def _prep_weights(g_mix, w_in, b_gate, g_qlat, w_q_up, g_kvlat, w_kv_up, g_qn, g_kn, w_a, w_f,
                  w_o, g_ffn, w_router, b_router, w_e_gate, w_e_up, w_e_down, w_s_gate, w_s_up,
                  w_s_down, g_ple, w_ple_gate, w_ple_proj):
    half = QK_ROPE // 2
    off_kv, off_kr = Q_LORA, Q_LORA + KV_LORA
    off_f = off_kr + QK_ROPE
    off_g = off_f + FNET_W
    w1 = jnp.concatenate([
        w_in[:, :off_kr],
        jnp.pad(w_in[:, off_kr:off_f], ((0, 0), (0, LANES - QK_ROPE))),
        w_in[:, off_f:]], axis=1).astype(BF16)
    wq3 = w_q_up.reshape(Q_LORA, N_HEADS, QK_HEAD)
    wq = _head_pad_cols(w_q_up, QK_HEAD).astype(BF16)
    wqr = _head_pad_cols(_rot_partner(wq3).reshape(Q_LORA, -1), QK_HEAD).astype(BF16)
    wkv3 = w_kv_up.reshape(KV_LORA, N_HEADS, QK_NOPE + V_HEAD)
    wk = _head_pad_cols(wkv3[:, :, :QK_NOPE].reshape(KV_LORA, -1), QK_NOPE).astype(BF16)
    wv = _head_pad_cols(wkv3[:, :, QK_NOPE:].reshape(KV_LORA, -1), V_HEAD).astype(BF16)
    vone = jnp.tile(jnp.concatenate([jnp.zeros((V_HEAD,), F32), jnp.ones((HEAD_PAD - V_HEAD,), F32)]),
                    N_HEADS).reshape(1, N_HEADS * HEAD_PAD)
    eye = jnp.eye(QK_ROPE, dtype=F32)
    place = jnp.concatenate([jnp.zeros((QK_ROPE, QK_NOPE), F32), eye], axis=1)
    place_rot = _rot_partner(place)
    ekr = jnp.concatenate([
        _head_pad_cols(jnp.tile(place, (1, N_HEADS)), QK_HEAD),
        _head_pad_cols(jnp.tile(place_rot, (1, N_HEADS)), QK_HEAD)], axis=1)
    ekr = jnp.pad(ekr, ((0, LANES - QK_ROPE), (0, 0))).astype(BF16)

    def gain_tabs(g):
        gp = jnp.concatenate([jnp.zeros((QK_NOPE,), F32), g[QK_NOPE + half:], g[QK_NOPE:QK_NOPE + half]])
        pad = lambda a: jnp.pad(a, (0, HEAD_PAD - QK_HEAD)).reshape(1, HEAD_PAD)
        return pad(g), pad(gp)

    gqc, gqs = gain_tabs(g_qn)
    gkc, gks = gain_tabs(g_kn)
    j = jnp.arange(FNET_GROUP_W, dtype=jnp.int32)
    ang = (2.0 * np.pi / FNET_GROUP_W) * ((j[:, None] * j[None, :]) % FNET_GROUP_W).astype(F32)
    eye_g = jnp.eye(FNET_GROUPS, dtype=F32)
    bd = jnp.concatenate([jnp.kron(eye_g, jnp.cos(ang)), jnp.kron(eye_g, jnp.sin(ang))], axis=1).astype(BF16)
    row = lambda a: a.reshape(1, -1).astype(F32)
    return {
        'g_mix': row(g_mix), 'w1': w1, 'b_gate': row(b_gate), 'g_qlat': row(g_qlat),
        'g_kvlat': row(g_kvlat), 'wq': wq, 'wqr': wqr, 'wk': wk, 'wv': wv, 'vone': vone, 'ekr': ekr,
        'gqc': gqc, 'gqs': gqs, 'gkc': gkc, 'gks': gks, 'bd': bd,
        'w_a': w_a.astype(BF16), 'w_f': w_f.astype(BF16), 'w_o': w_o.astype(BF16),
        'g_ffn': row(g_ffn),
        'w_sgu': jnp.concatenate([w_s_gate, w_s_up], axis=1).astype(BF16),
        'w_sd': w_s_down.astype(BF16),
        'w_rt': w_router.T.astype(F32), 'b_r': b_router.reshape(N_EXPERTS, 1).astype(F32),
        'w_eg': w_e_gate.astype(BF16), 'w_eu': w_e_up.astype(BF16), 'w_ed': w_e_down.astype(BF16),
        'g_ple': row(g_ple), 'w_pg': w_ple_gate.astype(BF16), 'w_pp': w_ple_proj.astype(BF16),
    }


def _tables(S):
    half = QK_ROPE // 2
    freqs = 1.0 / (ROPE_THETA ** (jnp.arange(half, dtype=F32) / half))
    ang = jnp.arange(S, dtype=F32)[:, None] * freqs[None, :]
    cos, sin = jnp.cos(ang), jnp.sin(ang)
    cos_t = jnp.concatenate([jnp.ones((S, QK_NOPE), F32), cos, cos,
                             jnp.zeros((S, HEAD_PAD - QK_HEAD), F32)], axis=1)
    sin_t = jnp.concatenate([jnp.zeros((S, QK_NOPE), F32), sin, sin,
                             jnp.zeros((S, HEAD_PAD - QK_HEAD), F32)], axis=1)
    split = 64
    j = jnp.arange(S, dtype=jnp.int32)[:, None]
    ka = jnp.arange(S // split, dtype=jnp.int32)[None, :] * split
    kb = jnp.arange(split, dtype=jnp.int32)[None, :]
    ang_a = (2.0 * np.pi / S) * ((j * ka) % S).astype(F32)
    ang_b = (2.0 * np.pi / S) * ((j * kb) % S).astype(F32)
    ca, sa = jnp.cos(ang_a)[:, :, None], jnp.sin(ang_a)[:, :, None]
    cb, sb = jnp.cos(ang_b)[:, None, :], jnp.sin(ang_b)[:, None, :]
    norm = (S * FNET_GROUP_W) ** -0.5
    dft_c = ((ca * cb - sa * sb) * norm).reshape(S, S).astype(BF16)
    dft_s = ((sa * cb + ca * sb) * -norm).reshape(S, S).astype(BF16)
    return {'cos': cos_t, 'sin': sin_t, 'dft_c': dft_c, 'dft_s': dft_s}


def _block(x, p, wts, tabs, *, tm, tq, tk, tdft, tmoe, chunk):
    B, S, D = x.shape
    T = B * S
    x2d = x.reshape(T, D)
    q, k, v, fcs, gates = _inproj(x2d, S, tabs, wts, tm)
    o = _attention(q.reshape(B, S, -1), k.reshape(B, S, -1), v.reshape(B, S, -1), tq, tk)
    bf = _seqdft(fcs.reshape(B, S, -1), tabs['dft_c'], tabs['dft_s'], tdft)
    hn, base, idx, w = _merge(x2d, o.reshape(T, -1), bf.reshape(T, -1), gates, wts, tm)
    tmoe = min(tmoe, T)
    cnt, start, tok_sorted, w_sorted = _dispatch(idx, w, tmoe, chunk)
    routed = _moe(hn, cnt, start, tok_sorted, w_sorted, wts, tmoe, chunk)
    y = _ple(base, routed, p.reshape(T, -1), wts, tm)
    return y.reshape(B, S, D)


def kernel(x_prompt, x_sample, p_prompt, p_sample, g_mix, w_in, b_gate, g_qlat, w_q_up, g_kvlat, w_kv_up, g_qn, g_kn, w_a, w_f, w_o, g_ffn, w_router, b_router, w_e_gate, w_e_up, w_e_down, w_s_gate, w_s_up, w_s_down, g_ple, w_ple_gate, w_ple_proj):
    params = (g_mix, w_in, b_gate, g_qlat, w_q_up, g_kvlat, w_kv_up, g_qn, g_kn, w_a, w_f, w_o,
              g_ffn, w_router, b_router, w_e_gate, w_e_up, w_e_down, w_s_gate, w_s_up, w_s_down,
              g_ple, w_ple_gate, w_ple_proj)
    depth = g_mix.shape[0]
    cfg = dict(tm=512, tq=512, tk=1024, tdft=512, tmoe=2048, chunk=288)
    xp, xs = x_prompt, x_sample
    tabs = _tables(x_prompt.shape[1])
    tabs_s = tabs if x_sample.shape[1] == x_prompt.shape[1] else _tables(x_sample.shape[1])
    for l in range(depth):
        wts = _prep_weights(*[a[l] for a in params])
        xp = _block(xp, p_prompt[l], wts, tabs, **cfg)
        xs = _block(xs, p_sample[l], wts, tabs_s, **cfg)
    return (xp, xs)
```

```python
import functools

import jax
import jax.numpy as jnp
import numpy as np
from jax import lax
from jax.experimental import pallas as pl
from jax.experimental.pallas import tpu as pltpu

D_MODEL = 1024
N_HEADS = 8
QK_NOPE = 64
QK_ROPE = 32
QK_HEAD = QK_NOPE + QK_ROPE
V_HEAD = 64
Q_LORA = 384
KV_LORA = 256
ROPE_THETA = 10000.0
FNET_GROUPS = 4
FNET_GROUP_W = 128
FNET_W = FNET_GROUPS * FNET_GROUP_W
N_EXPERTS = 64
TOP_K = 8
N_EXPERT_GROUPS = 8
TOPK_GROUPS = 4
EXPERT_FF = 256
SHARED_FF = 256
ROUTED_SCALE = 2.5
PLE_DIM = 256
EPS = 1e-6

LANES = 128
SUBLANES = 8
HEAD_PAD = LANES
ROWS_PER_TOKEN = D_MODEL // LANES
GROUP = 8
DOWN_SPLIT = 4
EXPERTS_PER_STEP = 2
HEADS_PER_STEP = 2

C_Q = 0
C_KV = C_Q + Q_LORA
C_KR = C_KV + KV_LORA
C_F = C_KR + LANES
C_G = C_F + FNET_W
C_END = C_G + 2 * D_MODEL

VMEM_LIMIT = 56 * 1024 * 1024
BF16 = jnp.bfloat16
F32 = jnp.float32


def _rms(x, n):
    return lax.rsqrt(jnp.sum(x * x, axis=-1, keepdims=True) * (1.0 / n) + EPS)


def _dot(a, b):
    return jnp.dot(a, b, preferred_element_type=F32)


def _rot_partner(x, lane):
    half = QK_ROPE // 2
    return jnp.where(lane < QK_NOPE + half, -pltpu.roll(x, HEAD_PAD - half, 1), pltpu.roll(x, half, 1))


def _inproj_kernel(x_ref, cos_ref, sin_ref, gmix_ref, w1_ref, bg_ref, gq_ref, gkv_ref,
                   wq_ref, wqr_ref, wk_ref, wv_ref, vone_ref, gqc_ref, gqs_ref, gkc_ref,
                   gks_ref, cs_ref, q_out, k_out, v_out, f_out, g_out):
    x = x_ref[...]
    tm = x.shape[0]
    h = (x * _rms(x, D_MODEL) * gmix_ref[...]).astype(BF16)
    proj = _dot(h, w1_ref[...])
    q_lat = proj[:, C_Q:C_KV]
    q_lat = (q_lat * _rms(q_lat, Q_LORA) * gq_ref[...]).astype(BF16)
    kv_lat = proj[:, C_KV:C_KR]
    kv_lat = (kv_lat * _rms(kv_lat, KV_LORA) * gkv_ref[...]).astype(BF16)
    kr = proj[:, C_KR:C_F]
    f_in = proj[:, C_F:C_G].astype(BF16)
    g_out[...] = jax.nn.sigmoid(proj[:, C_G:C_END] + bg_ref[...]).astype(BF16)
    for grp in range(FNET_GROUPS):
        gs = slice(grp * FNET_GROUP_W, (grp + 1) * FNET_GROUP_W)
        fcs = _dot(f_in[:, gs], cs_ref[...]).astype(BF16)
        f_out[:, gs] = fcs[:, :FNET_GROUP_W]
        f_out[:, FNET_W + grp * FNET_GROUP_W:FNET_W + (grp + 1) * FNET_GROUP_W] = fcs[:, FNET_GROUP_W:]
    v_out[...] = (_dot(kv_lat, wv_ref[...]) + vone_ref[...]).astype(BF16)

    q = _dot(q_lat, wq_ref[...])
    qr = _dot(q_lat, wqr_ref[...])
    k_nope = _dot(kv_lat, wk_ref[...])
    lane = lax.broadcasted_iota(jnp.int32, (tm, HEAD_PAD), 1)
    kr_placed = pltpu.roll(kr, QK_NOPE, 1)
    kr_partner = _rot_partner(kr_placed, lane)
    cos = cos_ref[...]
    sin = sin_ref[...]
    qc = gqc_ref[...] * cos
    qs = gqs_ref[...] * sin
    kc = gkc_ref[...] * cos
    ks = gks_ref[...] * sin
    scale = QK_HEAD ** -0.5 * np.log2(np.e)
    for hd in range(N_HEADS):
        sl = slice(hd * HEAD_PAD, (hd + 1) * HEAD_PAD)
        qh = q[:, sl]
        rq = _rms(qh, QK_HEAD) * scale
        q_out[:, sl] = ((qh * qc + qr[:, sl] * qs) * rq).astype(BF16)
        kh = k_nope[:, sl] + kr_placed
        rk = _rms(kh, QK_HEAD)
        k_out[:, sl] = ((kh * kc + kr_partner * ks) * rk).astype(BF16)


def _inproj(x2d, S, tabs, wts, tm):
    T = x2d.shape[0]
    n_pos = S // tm
    full = lambda a: pl.BlockSpec(a.shape, lambda i: (0,) * a.ndim)
    tok = lambda w: pl.BlockSpec((tm, w), lambda i: (i, 0))
    pos = pl.BlockSpec((tm, HEAD_PAD), lambda i: (i % n_pos, 0))
    names = ['g_mix', 'w1', 'b_gate', 'g_qlat', 'g_kvlat', 'wq', 'wqr', 'wk', 'wv', 'vone',
             'gqc', 'gqs', 'gkc', 'gks', 'cs128']
    ws = [wts[n] for n in names]
    return pl.pallas_call(
        _inproj_kernel,
        out_shape=(jax.ShapeDtypeStruct((T, N_HEADS * HEAD_PAD), BF16),
                   jax.ShapeDtypeStruct((T, N_HEADS * HEAD_PAD), BF16),
                   jax.ShapeDtypeStruct((T, N_HEADS * HEAD_PAD), BF16),
                   jax.ShapeDtypeStruct((T, 2 * FNET_W), BF16),
                   jax.ShapeDtypeStruct((T, 2 * D_MODEL), BF16)),
        grid=(T // tm,),
        in_specs=[tok(D_MODEL), pos, pos] + [full(w) for w in ws],
        out_specs=(tok(N_HEADS * HEAD_PAD), tok(N_HEADS * HEAD_PAD), tok(N_HEADS * HEAD_PAD),
                   tok(2 * FNET_W), tok(2 * D_MODEL)),
        compiler_params=pltpu.CompilerParams(dimension_semantics=("arbitrary",),
                                             vmem_limit_bytes=VMEM_LIMIT),
        name="inproj",
    )(x2d, tabs['cos'], tabs['sin'], *ws)


def _attn_kernel(q_ref, k_ref, v_ref, o_ref, s_a, s_b, *, tk):
    S = k_ref.shape[0]
    tq = q_ref.shape[0]
    nblk = tk // LANES
    n_chunks = S // tk
    heads = [slice(a * HEAD_PAD, (a + 1) * HEAD_PAD) for a in range(HEADS_PER_STEP)]
    s_bufs = (s_a, s_b)

    def scores(c, s_ref):
        for a, hs in enumerate(heads):
            s_ref[a] = lax.dot_general(q_ref[:, hs], k_ref[c * tk:(c + 1) * tk, hs],
                                       (((1,), (1,)), ((), ())), preferred_element_type=F32)

    def consume(c, s_ref, carry):
        new = []
        for a, hs in enumerate(heads):
            m, acc = carry[a]
            mx = s_ref[a, :, :LANES]
            for j in range(1, nblk):
                mx = jnp.maximum(mx, s_ref[a, :, j * LANES:(j + 1) * LANES])
            m_new = jnp.maximum(m, jnp.max(mx, axis=-1, keepdims=True))
            alpha = jnp.exp2(m - m_new)
            p = jnp.concatenate(
                [jnp.exp2((s_ref[a, :, j * LANES:(j + 1) * LANES] - m_new).astype(BF16))
                 for j in range(nblk)], axis=1)
            acc = alpha * acc + _dot(p, v_ref[c * tk:(c + 1) * tk, hs])
            new.append((m_new, acc))
        return new

    carry = [(jnp.full((tq, LANES), -jnp.inf, F32), jnp.zeros((tq, HEAD_PAD), F32))
             for _ in heads]
    scores(0, s_bufs[0])
    for c in range(n_chunks):
        if c + 1 < n_chunks:
            scores(c + 1, s_bufs[(c + 1) % 2])
        carry = consume(c, s_bufs[c % 2], carry)
    lane = lax.broadcasted_iota(jnp.int32, (tq, 2 * V_HEAD), 1)
    for pair in range(HEADS_PER_STEP // 2):
        acc_a, acc_b = carry[2 * pair][1], carry[2 * pair + 1][1]
        o_a = acc_a / pltpu.roll(acc_a, V_HEAD, 1)
        o_b = acc_b / pltpu.roll(acc_b, V_HEAD, 1)
        o_ref[:, pair * 2 * V_HEAD:(pair + 1) * 2 * V_HEAD] = jnp.where(
            lane < V_HEAD, o_a, pltpu.roll(o_b, V_HEAD, 1)).astype(o_ref.dtype)


def _attention(q, k, v, tq, tk):
    B, S, _ = q.shape
    hp = HEADS_PER_STEP
    return pl.pallas_call(
        functools.partial(_attn_kernel, tk=tk),
        out_shape=jax.ShapeDtypeStruct((B, S, N_HEADS * V_HEAD), BF16),
        grid=(B, N_HEADS // hp, S // tq),
        in_specs=[pl.BlockSpec((None, tq, hp * HEAD_PAD), lambda b, p, i: (b, i, p)),
                  pl.BlockSpec((None, S, hp * HEAD_PAD), lambda b, p, i: (b, 0, p)),
                  pl.BlockSpec((None, S, hp * HEAD_PAD), lambda b, p, i: (b, 0, p))],
        out_specs=pl.BlockSpec((None, tq, hp * V_HEAD), lambda b, p, i: (b, i, p)),
        scratch_shapes=[pltpu.VMEM((hp, tq, tk), F32), pltpu.VMEM((hp, tq, tk), F32)],
        compiler_params=pltpu.CompilerParams(
            dimension_semantics=("arbitrary", "arbitrary", "arbitrary"),
            vmem_limit_bytes=VMEM_LIMIT),
        name="attention",
    )(q, k, v)


def _seqdft_kernel(c_ref, s_ref, f_ref, o_ref):
    y = _dot(c_ref[...], f_ref[:, :FNET_W]) + _dot(s_ref[...], f_ref[:, FNET_W:])
    o_ref[...] = y.astype(o_ref.dtype)


def _seqdft(fcs, ctab, stab, tm):
    B, S, _ = fcs.shape
    return pl.pallas_call(
        _seqdft_kernel,
        out_shape=jax.ShapeDtypeStruct((B, S, FNET_W), BF16),
        grid=(S // tm, B),
        in_specs=[pl.BlockSpec((tm, S), lambda i, b: (i, 0)),
                  pl.BlockSpec((tm, S), lambda i, b: (i, 0)),
                  pl.BlockSpec((None, S, 2 * FNET_W), lambda i, b: (b, 0, 0))],
        out_specs=pl.BlockSpec((None, tm, FNET_W), lambda i, b: (b, i, 0)),
        compiler_params=pltpu.CompilerParams(dimension_semantics=("arbitrary", "arbitrary"),
                                             vmem_limit_bytes=VMEM_LIMIT),
        name="seqdft",
    )(ctab, stab, fcs)


def _first_argmax(v, iota, n):
    m = jnp.max(v, axis=0, keepdims=True)
    i = jnp.min(jnp.where(v == m, iota, n), axis=0, keepdims=True)
    return m, i


def _route(logits_t, bias):
    tm = logits_t.shape[1]
    per = N_EXPERTS // N_EXPERT_GROUPS
    scores = jax.nn.sigmoid(logits_t)
    sel = scores + bias
    neg = jnp.float32(-jnp.inf)
    iota_p = lax.broadcasted_iota(jnp.int32, (per, tm), 0)
    iota_g = lax.broadcasted_iota(jnp.int32, (N_EXPERT_GROUPS, tm), 0)
    gscore = jnp.zeros((N_EXPERT_GROUPS, tm), F32)
    for g in range(N_EXPERT_GROUPS):
        v = sel[g * per:(g + 1) * per, :]
        m1, i1 = _first_argmax(v, iota_p, per)
        m2 = jnp.max(jnp.where(iota_p == i1, neg, v), axis=0, keepdims=True)
        gscore = jnp.where(iota_g == g, m1 + m2, gscore)
    gmask = jnp.zeros((N_EXPERT_GROUPS, tm), jnp.bool_)
    v = gscore
    for _ in range(TOPK_GROUPS):
        _, i = _first_argmax(v, iota_g, N_EXPERT_GROUPS)
        hit = iota_g == i
        gmask = jnp.logical_or(gmask, hit)
        v = jnp.where(hit, neg, v)
    gkeep = jnp.where(gmask, 1.0, 0.0)
    parts = []
    for g in range(N_EXPERT_GROUPS):
        keep = gkeep[g:g + 1, :] > 0.5
        parts.append(jnp.where(keep, sel[g * per:(g + 1) * per, :], neg))
    v = jnp.concatenate(parts, axis=0)
    iota_e = lax.broadcasted_iota(jnp.int32, (N_EXPERTS, tm), 0)
    iota_k = lax.broadcasted_iota(jnp.int32, (TOP_K, tm), 0)
    idx = jnp.zeros((TOP_K, tm), jnp.int32)
    wts = jnp.zeros((TOP_K, tm), F32)
    for kk in range(TOP_K):
        _, i = _first_argmax(v, iota_e, N_EXPERTS)
        hit = iota_e == i
        wk = jnp.sum(jnp.where(hit, scores, 0.0), axis=0, keepdims=True)
        v = jnp.where(hit, neg, v)
        idx = jnp.where(iota_k == kk, i, idx)
        wts = jnp.where(iota_k == kk, wk, wts)
    wts = wts / jnp.sum(wts, axis=0, keepdims=True) * ROUTED_SCALE
    return idx, wts


def _merge_kernel(x_ref, o_ref, bf_ref, g_ref, wa_ref, wf_ref, wo_ref, gffn_ref, wsgu_ref,
                  wsd_ref, wrt_ref, br_ref, hn_out, base_out, idx_out, w_out):
    a = _dot(o_ref[...], wa_ref[...])
    b = _dot(bf_ref[...], wf_ref[...])
    g = g_ref[...].astype(F32)
    m = (g[:, :D_MODEL] * a + g[:, D_MODEL:] * b).astype(BF16)
    x2 = x_ref[...] + _dot(m, wo_ref[...])
    hn = x2 * _rms(x2, D_MODEL) * gffn_ref[...]
    tm = hn.shape[0]
    for r in range(ROWS_PER_TOKEN):
        hn_out[pl.ds(r, tm, stride=ROWS_PER_TOKEN), :] = hn[:, r * LANES:(r + 1) * LANES]
    hb = hn.astype(BF16)
    gu = _dot(hb, wsgu_ref[...])
    act = (jax.nn.silu(gu[:, :SHARED_FF]) * gu[:, SHARED_FF:]).astype(BF16)
    base_out[...] = x2 + _dot(act, wsd_ref[...])
    h_lo = (hn - hb.astype(F32)).astype(BF16)
    h3 = jnp.concatenate([hb, hb, h_lo], axis=1)
    logits_t = lax.dot_general(wrt_ref[...], h3, (((1,), (1,)), ((), ())),
                               preferred_element_type=F32)
    idx, wts = _route(logits_t, br_ref[...])
    idx_out[...] = idx
    w_out[...] = wts


def _merge(x2d, o2d, bf2d, gates, wts, tm):
    T = x2d.shape[0]
    full = lambda a: pl.BlockSpec(a.shape, lambda i: (0,) * a.ndim)
    tok = lambda w: pl.BlockSpec((tm, w), lambda i: (i, 0))
    names = ['w_a', 'w_f', 'w_o', 'g_ffn', 'w_sgu', 'w_sd', 'w_rt', 'b_r']
    ws = [wts[n] for n in names]
    return pl.pallas_call(
        _merge_kernel,
        out_shape=(jax.ShapeDtypeStruct((T * ROWS_PER_TOKEN, LANES), F32),
                   jax.ShapeDtypeStruct((T, D_MODEL), F32),
                   jax.ShapeDtypeStruct((TOP_K, T), jnp.int32),
                   jax.ShapeDtypeStruct((TOP_K, T), F32)),
        grid=(T // tm,),
        in_specs=[tok(D_MODEL), tok(N_HEADS * V_HEAD), tok(FNET_W), tok(2 * D_MODEL)]
                 + [full(w) for w in ws],
        out_specs=(pl.BlockSpec((tm * ROWS_PER_TOKEN, LANES), lambda i: (i, 0)), tok(D_MODEL),
                   pl.BlockSpec((TOP_K, tm), lambda i: (0, i)),
                   pl.BlockSpec((TOP_K, tm), lambda i: (0, i))),
        compiler_params=pltpu.CompilerParams(dimension_semantics=("arbitrary",),
                                             vmem_limit_bytes=VMEM_LIMIT),
        name="merge",
    )(x2d, o2d, bf2d, gates, *ws)


def _moe_kernel(cnt_ref, start_ref, tok_hbm, w_hbm, hn_ref, wg_ref, wu_ref, wd_ref, out_ref,
                tok_s, w_s, *scratch, chunk, static_scatter):
    E = EXPERTS_PER_STEP
    g_bufs = (scratch[0:E], scratch[E:2 * E])
    y_bufs = (scratch[2 * E:3 * E], scratch[3 * E:4 * E])
    sem = scratch[4 * E]
    i = pl.program_id(0)
    p = pl.program_id(1)
    n_groups = N_EXPERTS // E
    R = ROWS_PER_TOKEN
    first = i * N_EXPERTS + p * E

    def tile_at(row):
        return pl.ds(pl.multiple_of(row * R, R), R)

    def token_tile(off):
        return pl.ds(pl.multiple_of(off, R), R)

    def gather_group(base, j0, gbuf):
        vals = [hn_ref[token_tile(tok_s[base + j0 + r]), :] for r in range(GROUP)]
        for r in range(GROUP):
            gbuf[tile_at(j0 + r), :] = vals[r]

    def scatter_group(base, j0, ybuf):
        dsts = [token_tile(tok_s[base + j0 + r]) for r in range(GROUP)]
        vals = [out_ref[dsts[r], :] + w_s[base + j0 + r] * ybuf[tile_at(j0 + r), :]
                for r in range(GROUP)]
        for r in range(GROUP):
            out_ref[dsts[r], :] = vals[r]

    def gather_rolled(base, rows, gbuf):
        def group(g, _):
            gather_group(base, g * GROUP, gbuf)
            return 0

        def row(j, _):
            gbuf[tile_at(j), :] = hn_ref[token_tile(tok_s[base + j]), :]
            return 0

        lax.fori_loop(0, rows // GROUP, group, 0)
        lax.fori_loop(rows // GROUP * GROUP, rows, row, 0)

    def scatter_rolled(base, first_row, rows, ybuf):
        def group(g, _):
            scatter_group(base, g * GROUP, ybuf)
            return 0

        def row(j, _):
            dst = token_tile(tok_s[base + j])
            out_ref[dst, :] = out_ref[dst, :] + w_s[base + j] * ybuf[tile_at(j), :]
            return 0

        lax.fori_loop(first_row // GROUP, rows // GROUP, group, 0)
        lax.fori_loop(jnp.maximum(rows // GROUP * GROUP, first_row), rows, row, 0)

    def matmul_rows(gbuf):
        return jnp.concatenate([gbuf[pl.ds(r, chunk, stride=R), :] for r in range(R)],
                               axis=1).astype(BF16)

    def ffn_out(k, act, ybuf, col0, ncols):
        y = _dot(act, wd_ref[k, :, col0:col0 + ncols])
        for r in range(ncols // LANES):
            ybuf[pl.ds(col0 // LANES + r, chunk, stride=R), :] = y[:, r * LANES:(r + 1) * LANES]

    def ffn(k, gbuf, ybuf):
        xg = matmul_rows(gbuf)
        act = (jax.nn.silu(_dot(xg, wg_ref[k])) * _dot(xg, wu_ref[k])).astype(BF16)
        ffn_out(k, act, ybuf, 0, D_MODEL)

    ks = range(E)
    n = [cnt_ref[first + k] for k in ks]
    st = [start_ref[first + k] for k in ks]
    rows = [jnp.minimum(n[k], chunk) for k in ks]
    st_next = [start_ref[first + E + k] for k in ks]
    prev = [jnp.maximum(first - E + k, 0) for k in ks]
    st_prev = [start_ref[prev[k]] for k in ks]
    rows_prev = [jnp.where(p > 0, jnp.minimum(cnt_ref[prev[k]], chunk), 0) for k in ks]
    min_rows_prev = functools.reduce(jnp.minimum, rows_prev)

    @pl.when(jnp.logical_and(i == 0, p == 0))
    def _():
        for buf in g_bufs[0] + g_bufs[1]:
            buf[...] = jnp.zeros_like(buf)

    @pl.when(p == 0)
    def _():
        cp_t = pltpu.make_async_copy(tok_hbm.at[i], tok_s, sem.at[0])
        cp_w = pltpu.make_async_copy(w_hbm.at[i], w_s, sem.at[1])
        cp_t.start()
        cp_w.start()
        out_ref[...] = jnp.zeros_like(out_ref)
        cp_t.wait()
        cp_w.wait()
        for k in ks:
            gather_rolled(st[k], rows[k], g_bufs[0][k])

    def pipelined_step(parity):
        g_cur, g_next = g_bufs[parity], g_bufs[1 - parity]
        y_cur, y_prev = y_bufs[parity], y_bufs[1 - parity]

        @pl.when(min_rows_prev >= static_scatter)
        def _():
            groups = []
            for k in ks:
                sc = [functools.partial(scatter_group, st_prev[k], j0, y_prev[k])
                      for j0 in range(0, static_scatter, GROUP)]
                ga = [functools.partial(gather_group, st_next[k], j0, g_next[k])
                      for j0 in range(0, chunk, GROUP)]
                while sc or ga:
                    if sc:
                        groups.append(sc.pop(0))
                    if ga:
                        groups.append(ga.pop(0))
            n_stages = (2 + DOWN_SPLIT) * E
            per_stage = -(-len(groups) // n_stages)

            def emit_groups():
                for _ in range(min(per_stage, len(groups))):
                    groups.pop(0)()

            acts = []
            for k in ks:
                xg = matmul_rows(g_cur[k])
                g = _dot(xg, wg_ref[k])
                emit_groups()
                u = _dot(xg, wu_ref[k])
                emit_groups()
                acts.append((jax.nn.silu(g) * u).astype(BF16))
            ncols = D_MODEL // DOWN_SPLIT
            for k in ks:
                for c in range(DOWN_SPLIT):
                    ffn_out(k, acts[k], y_cur[k], c * ncols, ncols)
                    emit_groups()
            assert not groups
            for k in ks:
                scatter_rolled(st_prev[k], static_scatter, rows_prev[k], y_prev[k])

        @pl.when(min_rows_prev < static_scatter)
        def _():
            for k in ks:
                gather_rolled(st_next[k], chunk, g_next[k])
                ffn(k, g_cur[k], y_cur[k])
                scatter_rolled(st_prev[k], 0, rows_prev[k], y_prev[k])

        @pl.when(p == n_groups - 1)
        def _():
            for k in ks:
                scatter_rolled(st[k], 0, rows[k], y_cur[k])

        for k in ks:
            def extra_chunk(c, _, k=k):
                base = st[k] + c * chunk
                rows_c = jnp.minimum(chunk, n[k] - c * chunk)
                gather_rolled(base, rows_c, g_cur[0])
                ffn(k, g_cur[0], y_prev[0])
                scatter_rolled(base, 0, rows_c, y_prev[0])
                return 0

            lax.fori_loop(1, pl.cdiv(n[k], chunk), extra_chunk, 0)

    @pl.when(p % 2 == 0)
    def _():
        pipelined_step(0)

    @pl.when(p % 2 == 1)
    def _():
        pipelined_step(1)


def _moe(hn_tiles, cnt, start, tok_sorted, w_sorted, wts, tmoe, chunk):
    R = ROWS_PER_TOKEN
    T = hn_tiles.shape[0] // R
    n_tiles = T // tmoe
    L = tok_sorted.shape[1]
    E = EXPERTS_PER_STEP
    wspec = lambda a: pl.BlockSpec((E,) + a.shape[1:], lambda i, p, *_: (p, 0, 0))
    tile_spec = pl.BlockSpec((tmoe * R, LANES), lambda i, p, *_: (i, 0))
    row_buf = pltpu.VMEM((chunk * R, LANES), F32)
    static_scatter = (chunk * 3 // 4) // GROUP * GROUP
    return pl.pallas_call(
        functools.partial(_moe_kernel, chunk=chunk, static_scatter=static_scatter),
        out_shape=jax.ShapeDtypeStruct((T * R, LANES), F32),
        grid_spec=pltpu.PrefetchScalarGridSpec(
            num_scalar_prefetch=2,
            grid=(n_tiles, N_EXPERTS // E),
            in_specs=[pl.BlockSpec(memory_space=pl.ANY),
                      pl.BlockSpec(memory_space=pl.ANY),
                      tile_spec,
                      wspec(wts['w_eg']), wspec(wts['w_eu']), wspec(wts['w_ed'])],
            out_specs=tile_spec,
            scratch_shapes=[pltpu.SMEM((L,), jnp.int32),
                            pltpu.SMEM((L,), F32)]
                           + [row_buf] * (4 * E)
                           + [pltpu.SemaphoreType.DMA((2,))]),
        compiler_params=pltpu.CompilerParams(dimension_semantics=("arbitrary", "arbitrary"),
                                             vmem_limit_bytes=VMEM_LIMIT),
        name="moe",
    )(cnt, start, tok_sorted, w_sorted, hn_tiles, wts['w_eg'], wts['w_eu'], wts['w_ed'])


def _dispatch(idx, w, tmoe, chunk):
    T = idx.shape[1]
    n_tiles = T // tmoe
    per_tile = lambda a: a.reshape(TOP_K, n_tiles, tmoe).transpose(1, 0, 2).reshape(n_tiles, TOP_K * tmoe)
    e_t, w_t = per_tile(idx), per_tile(w)
    row_off = jnp.arange(tmoe, dtype=jnp.int32) * ROWS_PER_TOKEN
    tok_l = jnp.broadcast_to(jnp.tile(row_off, TOP_K)[None, :], e_t.shape)
    _, tok_sorted, w_sorted = lax.sort((e_t, tok_l, w_t), dimension=1, num_keys=1)
    onehot = e_t[..., None] == jnp.arange(N_EXPERTS, dtype=jnp.int32)
    cnt = jnp.sum(onehot.astype(jnp.int32), axis=1)
    start = jnp.cumsum(cnt, axis=1) - cnt
    pad_list = lambda a: jnp.pad(a, ((0, 0), (0, chunk)))
    pad_step = lambda a: jnp.pad(a.reshape(-1), (0, EXPERTS_PER_STEP))
    return pad_step(cnt), pad_step(start), pad_list(tok_sorted), pad_list(w_sorted)


def _ple_kernel(base_ref, routed_ref, p_ref, gple_ref, wg_ref, wp_ref, y_ref):
    tm = base_ref.shape[0]
    routed = jnp.concatenate([routed_ref[pl.ds(r, tm, stride=ROWS_PER_TOKEN), :]
                              for r in range(ROWS_PER_TOKEN)], axis=1)
    x3 = base_ref[...] + routed
    hn = (x3 * _rms(x3, D_MODEL) * gple_ref[...]).astype(BF16)
    gate = jax.nn.sigmoid(_dot(hn, wg_ref[...]))
    y_ref[...] = x3 + gate * _dot(p_ref[...].astype(BF16), wp_ref[...])


def _ple(base, routed, p2d, wts, tm):
    T = base.shape[0]
    full = lambda a: pl.BlockSpec(a.shape, lambda i: (0,) * a.ndim)
    tok = lambda w: pl.BlockSpec((tm, w), lambda i: (i, 0))
    ws = [wts['g_ple'], wts['w_pg'], wts['w_pp']]
    return pl.pallas_call(
        _ple_kernel,
        out_shape=jax.ShapeDtypeStruct((T, D_MODEL), F32),
        grid=(T // tm,),
        in_specs=[tok(D_MODEL), pl.BlockSpec((tm * ROWS_PER_TOKEN, LANES), lambda i: (i, 0)),
                  tok(PLE_DIM)] + [full(w) for w in ws],
        out_specs=tok(D_MODEL),
        compiler_params=pltpu.CompilerParams(dimension_semantics=("arbitrary",),
                                             vmem_limit_bytes=VMEM_LIMIT),
        name="ple",
    )(base, routed, p2d, *ws)


def _head_pad_cols(w, width):
    K = w.shape[0]
    w = w.reshape(K, N_HEADS, width)
    return jnp.pad(w, ((0, 0), (0, 0), (0, HEAD_PAD - width))).reshape(K, N_HEADS * HEAD_PAD)


def _prep_weights(g_mix, w_in, b_gate, g_qlat, w_q_up, g_kvlat, w_kv_up, g_qn, g_kn, w_a, w_f,
                  w_o, g_ffn, w_router, b_router, w_e_gate, w_e_up, w_e_down, w_s_gate, w_s_up,
                  w_s_down, g_ple, w_ple_gate, w_ple_proj):
    half = QK_ROPE // 2
    off_kv, off_kr = Q_LORA, Q_LORA + KV_LORA
    off_f = off_kr + QK_ROPE
    off_g = off_f + FNET_W
    w1 = jnp.concatenate([
        w_in[:, :off_kr],
        jnp.pad(w_in[:, off_kr:off_f], ((0, 0), (0, LANES - QK_ROPE))),
        w_in[:, off_f:]], axis=1).astype(BF16)
    wq = _head_pad_cols(w_q_up, QK_HEAD).astype(BF16)
    wq3 = w_q_up.reshape(Q_LORA, N_HEADS, QK_HEAD)
    wq3r = jnp.concatenate([jnp.zeros_like(wq3[..., :QK_NOPE]), -wq3[..., QK_NOPE + half:],
                            wq3[..., QK_NOPE:QK_NOPE + half]], axis=-1)
    wqr = _head_pad_cols(wq3r.reshape(Q_LORA, -1), QK_HEAD).astype(BF16)
    wkv3 = w_kv_up.reshape(KV_LORA, N_HEADS, QK_NOPE + V_HEAD)
    wk = _head_pad_cols(wkv3[:, :, :QK_NOPE].reshape(KV_LORA, -1), QK_NOPE).astype(BF16)
    wv = _head_pad_cols(wkv3[:, :, QK_NOPE:].reshape(KV_LORA, -1), V_HEAD).astype(BF16)
    vone = jnp.tile(jnp.concatenate([jnp.zeros((V_HEAD,), F32), jnp.ones((HEAD_PAD - V_HEAD,), F32)]),
                    N_HEADS).reshape(1, N_HEADS * HEAD_PAD)

    def gain_tabs(g):
        gp = jnp.concatenate([jnp.zeros((QK_NOPE,), F32), g[QK_NOPE + half:], g[QK_NOPE:QK_NOPE + half]])
        pad = lambda a: jnp.pad(a, (0, HEAD_PAD - QK_HEAD)).reshape(1, HEAD_PAD)
        return pad(g), pad(gp)

    gqc, gqs = gain_tabs(g_qn)
    gkc, gks = gain_tabs(g_kn)
    j = jnp.arange(FNET_GROUP_W, dtype=jnp.int32)
    ang = (2.0 * np.pi / FNET_GROUP_W) * ((j[:, None] * j[None, :]) % FNET_GROUP_W).astype(F32)
    cs128 = jnp.concatenate([jnp.cos(ang), jnp.sin(ang)], axis=1).astype(BF16)
    w_rt = w_router.T.astype(F32)
    w_rt_hi = w_rt.astype(BF16)
    w_rt_lo = (w_rt - w_rt_hi.astype(F32)).astype(BF16)
    w_rt3 = jnp.concatenate([w_rt_hi, w_rt_lo, w_rt_hi], axis=1)
    row = lambda a: a.reshape(1, -1).astype(F32)
    return {
        'g_mix': row(g_mix), 'w1': w1, 'b_gate': row(b_gate), 'g_qlat': row(g_qlat),
        'g_kvlat': row(g_kvlat), 'wq': wq, 'wqr': wqr, 'wk': wk, 'wv': wv, 'vone': vone,
        'gqc': gqc, 'gqs': gqs, 'gkc': gkc, 'gks': gks, 'cs128': cs128,
        'w_a': w_a.astype(BF16), 'w_f': w_f.astype(BF16), 'w_o': w_o.astype(BF16),
        'g_ffn': row(g_ffn),
        'w_sgu': jnp.concatenate([w_s_gate, w_s_up], axis=1).astype(BF16),
        'w_sd': w_s_down.astype(BF16),
        'w_rt': w_rt3, 'b_r': b_router.reshape(N_EXPERTS, 1).astype(F32),
        'w_eg': w_e_gate.astype(BF16), 'w_eu': w_e_up.astype(BF16), 'w_ed': w_e_down.astype(BF16),
        'g_ple': row(g_ple), 'w_pg': w_ple_gate.astype(BF16), 'w_pp': w_ple_proj.astype(BF16),
    }


def _tables(S):
    half = QK_ROPE // 2
    freqs = 1.0 / (ROPE_THETA ** (jnp.arange(half, dtype=F32) / half))
    ang = jnp.arange(S, dtype=F32)[:, None] * freqs[None, :]
    cos, sin = jnp.cos(ang), jnp.sin(ang)
    cos_t = jnp.concatenate([jnp.ones((S, QK_NOPE), F32), cos, cos,
                             jnp.zeros((S, HEAD_PAD - QK_HEAD), F32)], axis=1)
    sin_t = jnp.concatenate([jnp.zeros((S, QK_NOPE), F32), sin, sin,
                             jnp.zeros((S, HEAD_PAD - QK_HEAD), F32)], axis=1)
    split = 64
    j = jnp.arange(S, dtype=jnp.int32)[:, None]
    ka = jnp.arange(S // split, dtype=jnp.int32)[None, :] * split
    kb = jnp.arange(split, dtype=jnp.int32)[None, :]
    ang_a = (2.0 * np.pi / S) * ((j * ka) % S).astype(F32)
    ang_b = (2.0 * np.pi / S) * ((j * kb) % S).astype(F32)
    ca, sa = jnp.cos(ang_a)[:, :, None], jnp.sin(ang_a)[:, :, None]
    cb, sb = jnp.cos(ang_b)[:, None, :], jnp.sin(ang_b)[:, None, :]
    norm = (S * FNET_GROUP_W) ** -0.5
    dft_c = ((ca * cb - sa * sb) * norm).reshape(S, S).astype(BF16)
    dft_s = ((sa * cb + ca * sb) * -norm).reshape(S, S).astype(BF16)
    return {'cos': cos_t, 'sin': sin_t, 'dft_c': dft_c, 'dft_s': dft_s}


def _block(x, p, wts, tabs, *, tm, tq, tk, tdft, tmoe, chunk):
    B, S, D = x.shape
    T = B * S
    x2d = x.reshape(T, D)
    q, k, v, fcs, gates = _inproj(x2d, S, tabs, wts, tm)
    o = _attention(q.reshape(B, S, -1), k.reshape(B, S, -1), v.reshape(B, S, -1), tq, tk)
    bf = _seqdft(fcs.reshape(B, S, -1), tabs['dft_c'], tabs['dft_s'], tdft)
    hn, base, idx, w = _merge(x2d, o.reshape(T, -1), bf.reshape(T, -1), gates, wts, tm)
    tmoe = min(tmoe, T)
    cnt, start, tok_sorted, w_sorted = _dispatch(idx, w, tmoe, chunk)
    routed = _moe(hn, cnt, start, tok_sorted, w_sorted, wts, tmoe, chunk)
    y = _ple(base, routed, p.reshape(T, -1), wts, tm)
    return y.reshape(B, S, D)


def kernel(x_prompt, x_sample, p_prompt, p_sample, g_mix, w_in, b_gate, g_qlat, w_q_up, g_kvlat, w_kv_up, g_qn, g_kn, w_a, w_f, w_o, g_ffn, w_router, b_router, w_e_gate, w_e_up, w_e_down, w_s_gate, w_s_up, w_s_down, g_ple, w_ple_gate, w_ple_proj):
    params = (g_mix, w_in, b_gate, g_qlat, w_q_up, g_kvlat, w_kv_up, g_qn, g_kn, w_a, w_f, w_o,
              g_ffn, w_router, b_router, w_e_gate, w_e_up, w_e_down, w_s_gate, w_s_up, w_s_down,
              g_ple, w_ple_gate, w_ple_proj)
    depth = g_mix.shape[0]
    cfg = dict(tm=512, tq=512, tk=1024, tdft=512, tmoe=2048, chunk=288)
    xp, xs = x_prompt, x_sample
    tabs = _tables(x_prompt.shape[1])
    tabs_s = tabs if x_sample.shape[1] == x_prompt.shape[1] else _tables(x_sample.shape[1])
    for l in range(depth):
        wts = _prep_weights(*[a[l] for a in params])
        xp = _block(xp, p_prompt[l], wts, tabs, **cfg)
        xs = _block(xs, p_sample[l], wts, tabs_s, **cfg)
    return (xp, xs)
```

```python
import functools

import jax
import jax.numpy as jnp
import numpy as np
from jax import lax
from jax.experimental import pallas as pl
from jax.experimental.pallas import tpu as pltpu

D_MODEL = 1024
N_HEADS = 8
QK_NOPE = 64
QK_ROPE = 32
QK_HEAD = QK_NOPE + QK_ROPE
V_HEAD = 64
Q_LORA = 384
KV_LORA = 256
ROPE_THETA = 10000.0
FNET_GROUPS = 4
FNET_GROUP_W = 128
FNET_W = FNET_GROUPS * FNET_GROUP_W
N_EXPERTS = 64
TOP_K = 8
N_EXPERT_GROUPS = 8
TOPK_GROUPS = 4
EXPERT_FF = 256
SHARED_FF = 256
ROUTED_SCALE = 2.5
PLE_DIM = 256
EPS = 1e-6

LANES = 128
SUBLANES = 8
HEAD_PAD = LANES
ROWS_PER_TOKEN = D_MODEL // LANES
GROUP = 8
DOWN_SPLIT = 2
EXPERTS_PER_STEP = 2
HEADS_PER_STEP = 4

C_Q = 0
C_KV = C_Q + Q_LORA
C_KR = C_KV + KV_LORA
C_F = C_KR + LANES
C_G = C_F + FNET_W
C_END = C_G + 2 * D_MODEL

VMEM_LIMIT = 56 * 1024 * 1024
BF16 = jnp.bfloat16
F32 = jnp.float32


def _rms(x, n):
    return lax.rsqrt(jnp.sum(x * x, axis=-1, keepdims=True) * (1.0 / n) + EPS)


def _dot(a, b):
    return jnp.dot(a, b, preferred_element_type=F32)


def _rot_partner(x, lane):
    half = QK_ROPE // 2
    return jnp.where(lane < QK_NOPE + half, -pltpu.roll(x, HEAD_PAD - half, 1), pltpu.roll(x, half, 1))


def _inproj_kernel(x_ref, cos_ref, sin_ref, gmix_ref, w1_ref, bg_ref, gq_ref, gkv_ref,
                   wq_ref, wqr_ref, wk_ref, wv_ref, vone_ref, gqc_ref, gqs_ref, gkc_ref,
                   gks_ref, cs_ref, q_out, k_out, v_out, f_out, g_out):
    x = x_ref[...]
    tm = x.shape[0]
    h = (x * _rms(x, D_MODEL) * gmix_ref[...]).astype(BF16)
    proj = _dot(h, w1_ref[...])
    q_lat = proj[:, C_Q:C_KV]
    q_lat = (q_lat * _rms(q_lat, Q_LORA) * gq_ref[...]).astype(BF16)
    kv_lat = proj[:, C_KV:C_KR]
    kv_lat = (kv_lat * _rms(kv_lat, KV_LORA) * gkv_ref[...]).astype(BF16)
    kr = proj[:, C_KR:C_F]
    f_in = proj[:, C_F:C_G].astype(BF16)
    g_out[...] = jax.nn.sigmoid(proj[:, C_G:C_END] + bg_ref[...]).astype(BF16)
    for grp in range(FNET_GROUPS):
        gs = slice(grp * FNET_GROUP_W, (grp + 1) * FNET_GROUP_W)
        fcs = _dot(f_in[:, gs], cs_ref[...]).astype(BF16)
        f_out[:, gs] = fcs[:, :FNET_GROUP_W]
        f_out[:, FNET_W + grp * FNET_GROUP_W:FNET_W + (grp + 1) * FNET_GROUP_W] = fcs[:, FNET_GROUP_W:]
    v_out[...] = (_dot(kv_lat, wv_ref[...]) + vone_ref[...]).astype(BF16)

    q = _dot(q_lat, wq_ref[...])
    qr = _dot(q_lat, wqr_ref[...])
    k_nope = _dot(kv_lat, wk_ref[...])
    lane = lax.broadcasted_iota(jnp.int32, (tm, HEAD_PAD), 1)
    kr_placed = pltpu.roll(kr, QK_NOPE, 1)
    kr_partner = _rot_partner(kr_placed, lane)
    cos = cos_ref[...]
    sin = sin_ref[...]
    qc = gqc_ref[...] * cos
    qs = gqs_ref[...] * sin
    kc = gkc_ref[...] * cos
    ks = gks_ref[...] * sin
    scale = QK_HEAD ** -0.5 * np.log2(np.e)
    for hd in range(N_HEADS):
        sl = slice(hd * HEAD_PAD, (hd + 1) * HEAD_PAD)
        qh = q[:, sl]
        rq = _rms(qh, QK_HEAD) * scale
        q_out[:, sl] = ((qh * qc + qr[:, sl] * qs) * rq).astype(BF16)
        kh = k_nope[:, sl] + kr_placed
        rk = _rms(kh, QK_HEAD)
        k_out[:, sl] = ((kh * kc + kr_partner * ks) * rk).astype(BF16)


def _inproj(x2d, S, tabs, wts, tm):
    T = x2d.shape[0]
    n_pos = S // tm
    full = lambda a: pl.BlockSpec(a.shape, lambda i: (0,) * a.ndim)
    tok = lambda w: pl.BlockSpec((tm, w), lambda i: (i, 0))
    pos = pl.BlockSpec((tm, HEAD_PAD), lambda i: (i % n_pos, 0))
    names = ['g_mix', 'w1', 'b_gate', 'g_qlat', 'g_kvlat', 'wq', 'wqr', 'wk', 'wv', 'vone',
             'gqc', 'gqs', 'gkc', 'gks', 'cs128']
    ws = [wts[n] for n in names]
    return pl.pallas_call(
        _inproj_kernel,
        out_shape=(jax.ShapeDtypeStruct((T, N_HEADS * HEAD_PAD), BF16),
                   jax.ShapeDtypeStruct((T, N_HEADS * HEAD_PAD), BF16),
                   jax.ShapeDtypeStruct((T, N_HEADS * HEAD_PAD), BF16),
                   jax.ShapeDtypeStruct((T, 2 * FNET_W), BF16),
                   jax.ShapeDtypeStruct((T, 2 * D_MODEL), BF16)),
        grid=(T // tm,),
        in_specs=[tok(D_MODEL), pos, pos] + [full(w) for w in ws],
        out_specs=(tok(N_HEADS * HEAD_PAD), tok(N_HEADS * HEAD_PAD), tok(N_HEADS * HEAD_PAD),
                   tok(2 * FNET_W), tok(2 * D_MODEL)),
        compiler_params=pltpu.CompilerParams(dimension_semantics=("arbitrary",),
                                             vmem_limit_bytes=VMEM_LIMIT),
        name="inproj",
    )(x2d, tabs['cos'], tabs['sin'], *ws)


def _attn_kernel(q_ref, k_ref, v_ref, o_ref, s_a, s_b, *, tk):
    S = k_ref.shape[0]
    tq = q_ref.shape[0]
    nblk = tk // LANES
    n_chunks = S // tk
    heads = [slice(a * HEAD_PAD, (a + 1) * HEAD_PAD) for a in range(HEADS_PER_STEP)]
    s_bufs = (s_a, s_b)

    def scores(c, s_ref):
        for a, hs in enumerate(heads):
            s_ref[a] = lax.dot_general(q_ref[:, hs], k_ref[c * tk:(c + 1) * tk, hs],
                                       (((1,), (1,)), ((), ())), preferred_element_type=F32)

    def consume(c, s_ref, carry):
        new = []
        for a, hs in enumerate(heads):
            m, acc = carry[a]
            mx = s_ref[a, :, :LANES]
            for j in range(1, nblk):
                mx = jnp.maximum(mx, s_ref[a, :, j * LANES:(j + 1) * LANES])
            m_new = jnp.maximum(m, jnp.max(mx, axis=-1, keepdims=True))
            alpha = jnp.exp2(m - m_new)
            p = jnp.concatenate(
                [jnp.exp2((s_ref[a, :, j * LANES:(j + 1) * LANES] - m_new).astype(BF16))
                 for j in range(nblk)], axis=1)
            acc = alpha * acc + _dot(p, v_ref[c * tk:(c + 1) * tk, hs])
            new.append((m_new, acc))
        return new

    carry = [(jnp.full((tq, LANES), -jnp.inf, F32), jnp.zeros((tq, HEAD_PAD), F32))
             for _ in heads]
    scores(0, s_bufs[0])
    for c in range(n_chunks):
        if c + 1 < n_chunks:
            scores(c + 1, s_bufs[(c + 1) % 2])
        carry = consume(c, s_bufs[c % 2], carry)
    lane = lax.broadcasted_iota(jnp.int32, (tq, 2 * V_HEAD), 1)
    for pair in range(HEADS_PER_STEP // 2):
        acc_a, acc_b = carry[2 * pair][1], carry[2 * pair + 1][1]
        o_a = acc_a / pltpu.roll(acc_a, V_HEAD, 1)
        o_b = acc_b / pltpu.roll(acc_b, V_HEAD, 1)
        o_ref[:, pair * 2 * V_HEAD:(pair + 1) * 2 * V_HEAD] = jnp.where(
            lane < V_HEAD, o_a, pltpu.roll(o_b, V_HEAD, 1)).astype(o_ref.dtype)


def _attention(q, k, v, tq, tk):
    B, S, _ = q.shape
    hp = HEADS_PER_STEP
    return pl.pallas_call(
        functools.partial(_attn_kernel, tk=tk),
        out_shape=jax.ShapeDtypeStruct((B, S, N_HEADS * V_HEAD), BF16),
        grid=(B, N_HEADS // hp, S // tq),
        in_specs=[pl.BlockSpec((None, tq, hp * HEAD_PAD), lambda b, p, i: (b, i, p)),
                  pl.BlockSpec((None, S, hp * HEAD_PAD), lambda b, p, i: (b, 0, p)),
                  pl.BlockSpec((None, S, hp * HEAD_PAD), lambda b, p, i: (b, 0, p))],
        out_specs=pl.BlockSpec((None, tq, hp * V_HEAD), lambda b, p, i: (b, i, p)),
        scratch_shapes=[pltpu.VMEM((hp, tq, tk), F32), pltpu.VMEM((hp, tq, tk), F32)],
        compiler_params=pltpu.CompilerParams(
            dimension_semantics=("arbitrary", "arbitrary", "arbitrary"),
            vmem_limit_bytes=VMEM_LIMIT),
        name="attention",
    )(q, k, v)


def _seqdft_kernel(c_ref, s_ref, f_ref, o_ref):
    y = _dot(c_ref[...], f_ref[:, :FNET_W]) + _dot(s_ref[...], f_ref[:, FNET_W:])
    o_ref[...] = y.astype(o_ref.dtype)


def _seqdft(fcs, ctab, stab, tm):
    B, S, _ = fcs.shape
    return pl.pallas_call(
        _seqdft_kernel,
        out_shape=jax.ShapeDtypeStruct((B, S, FNET_W), BF16),
        grid=(S // tm, B),
        in_specs=[pl.BlockSpec((tm, S), lambda i, b: (i, 0)),
                  pl.BlockSpec((tm, S), lambda i, b: (i, 0)),
                  pl.BlockSpec((None, S, 2 * FNET_W), lambda i, b: (b, 0, 0))],
        out_specs=pl.BlockSpec((None, tm, FNET_W), lambda i, b: (b, i, 0)),
        compiler_params=pltpu.CompilerParams(dimension_semantics=("arbitrary", "arbitrary"),
                                             vmem_limit_bytes=VMEM_LIMIT),
        name="seqdft",
    )(ctab, stab, fcs)


def _first_argmax(v, iota, n):
    m = jnp.max(v, axis=0, keepdims=True)
    i = jnp.min(jnp.where(v == m, iota, n), axis=0, keepdims=True)
    return m, i


def _route(logits_t, bias):
    tm = logits_t.shape[1]
    per = N_EXPERTS // N_EXPERT_GROUPS
    scores = jax.nn.sigmoid(logits_t)
    sel = scores + bias
    neg = jnp.float32(-jnp.inf)
    iota_p = lax.broadcasted_iota(jnp.int32, (per, tm), 0)
    iota_g = lax.broadcasted_iota(jnp.int32, (N_EXPERT_GROUPS, tm), 0)
    gscore = jnp.zeros((N_EXPERT_GROUPS, tm), F32)
    for g in range(N_EXPERT_GROUPS):
        v = sel[g * per:(g + 1) * per, :]
        m1, i1 = _first_argmax(v, iota_p, per)
        m2 = jnp.max(jnp.where(iota_p == i1, neg, v), axis=0, keepdims=True)
        gscore = jnp.where(iota_g == g, m1 + m2, gscore)
    gmask = jnp.zeros((N_EXPERT_GROUPS, tm), jnp.bool_)
    v = gscore
    for _ in range(TOPK_GROUPS):
        _, i = _first_argmax(v, iota_g, N_EXPERT_GROUPS)
        hit = iota_g == i
        gmask = jnp.logical_or(gmask, hit)
        v = jnp.where(hit, neg, v)
    gkeep = jnp.where(gmask, 1.0, 0.0)
    parts = []
    for g in range(N_EXPERT_GROUPS):
        keep = gkeep[g:g + 1, :] > 0.5
        parts.append(jnp.where(keep, sel[g * per:(g + 1) * per, :], neg))
    v = jnp.concatenate(parts, axis=0)
    iota_e = lax.broadcasted_iota(jnp.int32, (N_EXPERTS, tm), 0)
    iota_k = lax.broadcasted_iota(jnp.int32, (TOP_K, tm), 0)
    idx = jnp.zeros((TOP_K, tm), jnp.int32)
    wts = jnp.zeros((TOP_K, tm), F32)
    for kk in range(TOP_K):
        _, i = _first_argmax(v, iota_e, N_EXPERTS)
        hit = iota_e == i
        wk = jnp.sum(jnp.where(hit, scores, 0.0), axis=0, keepdims=True)
        v = jnp.where(hit, neg, v)
        idx = jnp.where(iota_k == kk, i, idx)
        wts = jnp.where(iota_k == kk, wk, wts)
    wts = wts / jnp.sum(wts, axis=0, keepdims=True) * ROUTED_SCALE
    return idx, wts


def _merge_kernel(x_ref, o_ref, bf_ref, g_ref, wa_ref, wf_ref, wo_ref, gffn_ref, wsgu_ref,
                  wsd_ref, wrt_ref, br_ref, hn_out, base_out, idx_out, w_out):
    a = _dot(o_ref[...], wa_ref[...])
    b = _dot(bf_ref[...], wf_ref[...])
    g = g_ref[...].astype(F32)
    m = (g[:, :D_MODEL] * a + g[:, D_MODEL:] * b).astype(BF16)
    x2 = x_ref[...] + _dot(m, wo_ref[...])
    hn = x2 * _rms(x2, D_MODEL) * gffn_ref[...]
    tm = hn.shape[0]
    for r in range(ROWS_PER_TOKEN):
        hn_out[pl.ds(r, tm, stride=ROWS_PER_TOKEN), :] = hn[:, r * LANES:(r + 1) * LANES]
    hb = hn.astype(BF16)
    gu = _dot(hb, wsgu_ref[...])
    act = (jax.nn.silu(gu[:, :SHARED_FF]) * gu[:, SHARED_FF:]).astype(BF16)
    base_out[...] = x2 + _dot(act, wsd_ref[...])
    h_lo = (hn - hb.astype(F32)).astype(BF16)
    h3 = jnp.concatenate([hb, hb, h_lo], axis=1)
    logits_t = lax.dot_general(wrt_ref[...], h3, (((1,), (1,)), ((), ())),
                               preferred_element_type=F32)
    idx, wts = _route(logits_t, br_ref[...])
    idx_out[...] = idx
    w_out[...] = wts


def _merge(x2d, o2d, bf2d, gates, wts, tm):
    T = x2d.shape[0]
    full = lambda a: pl.BlockSpec(a.shape, lambda i: (0,) * a.ndim)
    tok = lambda w: pl.BlockSpec((tm, w), lambda i: (i, 0))
    names = ['w_a', 'w_f', 'w_o', 'g_ffn', 'w_sgu', 'w_sd', 'w_rt', 'b_r']
    ws = [wts[n] for n in names]
    return pl.pallas_call(
        _merge_kernel,
        out_shape=(jax.ShapeDtypeStruct((T * ROWS_PER_TOKEN, LANES), F32),
                   jax.ShapeDtypeStruct((T, D_MODEL), F32),
                   jax.ShapeDtypeStruct((TOP_K, T), jnp.int32),
                   jax.ShapeDtypeStruct((TOP_K, T), F32)),
        grid=(T // tm,),
        in_specs=[tok(D_MODEL), tok(N_HEADS * V_HEAD), tok(FNET_W), tok(2 * D_MODEL)]
                 + [full(w) for w in ws],
        out_specs=(pl.BlockSpec((tm * ROWS_PER_TOKEN, LANES), lambda i: (i, 0)), tok(D_MODEL),
                   pl.BlockSpec((TOP_K, tm), lambda i: (0, i)),
                   pl.BlockSpec((TOP_K, tm), lambda i: (0, i))),
        compiler_params=pltpu.CompilerParams(dimension_semantics=("arbitrary",),
                                             vmem_limit_bytes=VMEM_LIMIT),
        name="merge",
    )(x2d, o2d, bf2d, gates, *ws)


def _moe_kernel(cnt_ref, start_ref, tok_hbm, w_hbm, hn_ref, wg_ref, wu_ref, wd_ref, out_ref,
                tok_s, w_s, *scratch, chunk, static_scatter):
    E = EXPERTS_PER_STEP
    g_bufs = (scratch[0:E], scratch[E:2 * E])
    y_bufs = (scratch[2 * E:3 * E], scratch[3 * E:4 * E])
    sem = scratch[4 * E]
    i = pl.program_id(0)
    p = pl.program_id(1)
    n_groups = N_EXPERTS // E
    R = ROWS_PER_TOKEN
    first = i * N_EXPERTS + p * E

    def tile_at(row):
        return pl.ds(pl.multiple_of(row * R, R), R)

    def token_tile(off):
        return pl.ds(pl.multiple_of(off, R), R)

    def gather_group(base, j0, gbuf):
        vals = [hn_ref[token_tile(tok_s[base + j0 + r]), :] for r in range(GROUP)]
        for r in range(GROUP):
            gbuf[tile_at(j0 + r), :] = vals[r]

    def scatter_group(base, j0, ybuf):
        dsts = [token_tile(tok_s[base + j0 + r]) for r in range(GROUP)]
        vals = [out_ref[dsts[r], :] + w_s[base + j0 + r] * ybuf[tile_at(j0 + r), :]
                for r in range(GROUP)]
        for r in range(GROUP):
            out_ref[dsts[r], :] = vals[r]

    def gather_rolled(base, rows, gbuf):
        def group(g, _):
            gather_group(base, g * GROUP, gbuf)
            return 0

        def row(j, _):
            gbuf[tile_at(j), :] = hn_ref[token_tile(tok_s[base + j]), :]
            return 0

        lax.fori_loop(0, rows // GROUP, group, 0)
        lax.fori_loop(rows // GROUP * GROUP, rows, row, 0)

    def scatter_rolled(base, first_row, rows, ybuf):
        def group(g, _):
            scatter_group(base, g * GROUP, ybuf)
            return 0

        def row(j, _):
            dst = token_tile(tok_s[base + j])
            out_ref[dst, :] = out_ref[dst, :] + w_s[base + j] * ybuf[tile_at(j), :]
            return 0

        lax.fori_loop(first_row // GROUP, rows // GROUP, group, 0)
        lax.fori_loop(jnp.maximum(rows // GROUP * GROUP, first_row), rows, row, 0)

    def matmul_rows(gbuf):
        return jnp.concatenate([gbuf[pl.ds(r, chunk, stride=R), :] for r in range(R)],
                               axis=1).astype(BF16)

    def ffn_out(k, act, ybuf, col0, ncols):
        y = _dot(act, wd_ref[k, :, col0:col0 + ncols])
        for r in range(ncols // LANES):
            ybuf[pl.ds(col0 // LANES + r, chunk, stride=R), :] = y[:, r * LANES:(r + 1) * LANES]

    def ffn(k, gbuf, ybuf):
        xg = matmul_rows(gbuf)
        act = (jax.nn.silu(_dot(xg, wg_ref[k])) * _dot(xg, wu_ref[k])).astype(BF16)
        ffn_out(k, act, ybuf, 0, D_MODEL)

    ks = range(E)
    n = [cnt_ref[first + k] for k in ks]
    st = [start_ref[first + k] for k in ks]
    rows = [jnp.minimum(n[k], chunk) for k in ks]
    st_next = [start_ref[first + E + k] for k in ks]
    prev = [jnp.maximum(first - E + k, 0) for k in ks]
    st_prev = [start_ref[prev[k]] for k in ks]
    rows_prev = [jnp.where(p > 0, jnp.minimum(cnt_ref[prev[k]], chunk), 0) for k in ks]
    min_rows_prev = functools.reduce(jnp.minimum, rows_prev)

    @pl.when(jnp.logical_and(i == 0, p == 0))
    def _():
        for buf in g_bufs[0] + g_bufs[1]:
            buf[...] = jnp.zeros_like(buf)

    @pl.when(p == 0)
    def _():
        cp_t = pltpu.make_async_copy(tok_hbm.at[i], tok_s, sem.at[0])
        cp_w = pltpu.make_async_copy(w_hbm.at[i], w_s, sem.at[1])
        cp_t.start()
        cp_w.start()
        out_ref[...] = jnp.zeros_like(out_ref)
        cp_t.wait()
        cp_w.wait()
        for k in ks:
            gather_rolled(st[k], rows[k], g_bufs[0][k])

    def pipelined_step(parity):
        g_cur, g_next = g_bufs[parity], g_bufs[1 - parity]
        y_cur, y_prev = y_bufs[parity], y_bufs[1 - parity]

        @pl.when(min_rows_prev >= static_scatter)
        def _():
            groups = []
            for k in ks:
                sc = [functools.partial(scatter_group, st_prev[k], j0, y_prev[k])
                      for j0 in range(0, static_scatter, GROUP)]
                ga = [functools.partial(gather_group, st_next[k], j0, g_next[k])
                      for j0 in range(0, chunk, GROUP)]
                while sc or ga:
                    if sc:
                        groups.append(sc.pop(0))
                    if ga:
                        groups.append(ga.pop(0))
            n_stages = (2 + DOWN_SPLIT) * E
            per_stage = -(-len(groups) // n_stages)

            def emit_groups():
                for _ in range(min(per_stage, len(groups))):
                    groups.pop(0)()

            acts = []
            for k in ks:
                xg = matmul_rows(g_cur[k])
                g = _dot(xg, wg_ref[k])
                emit_groups()
                u = _dot(xg, wu_ref[k])
                emit_groups()
                acts.append((jax.nn.silu(g) * u).astype(BF16))
            ncols = D_MODEL // DOWN_SPLIT
            for k in ks:
                for c in range(DOWN_SPLIT):
                    ffn_out(k, acts[k], y_cur[k], c * ncols, ncols)
                    emit_groups()
            assert not groups
            for k in ks:
                scatter_rolled(st_prev[k], static_scatter, rows_prev[k], y_prev[k])

        @pl.when(min_rows_prev < static_scatter)
        def _():
            for k in ks:
                gather_rolled(st_next[k], chunk, g_next[k])
                ffn(k, g_cur[k], y_cur[k])
                scatter_rolled(st_prev[k], 0, rows_prev[k], y_prev[k])

        @pl.when(p == n_groups - 1)
        def _():
            for k in ks:
                scatter_rolled(st[k], 0, rows[k], y_cur[k])

        for k in ks:
            def extra_chunk(c, _, k=k):
                base = st[k] + c * chunk
                rows_c = jnp.minimum(chunk, n[k] - c * chunk)
                gather_rolled(base, rows_c, g_cur[0])
                ffn(k, g_cur[0], y_prev[0])
                scatter_rolled(base, 0, rows_c, y_prev[0])
                return 0

            lax.fori_loop(1, pl.cdiv(n[k], chunk), extra_chunk, 0)

    @pl.when(p % 2 == 0)
    def _():
        pipelined_step(0)

    @pl.when(p % 2 == 1)
    def _():
        pipelined_step(1)


def _moe(hn_tiles, cnt, start, tok_sorted, w_sorted, wts, tmoe, chunk):
    R = ROWS_PER_TOKEN
    T = hn_tiles.shape[0] // R
    n_tiles = T // tmoe
    L = tok_sorted.shape[1]
    E = EXPERTS_PER_STEP
    wspec = lambda a: pl.BlockSpec((E,) + a.shape[1:], lambda i, p, *_: (p, 0, 0))
    tile_spec = pl.BlockSpec((tmoe * R, LANES), lambda i, p, *_: (i, 0))
    row_buf = pltpu.VMEM((chunk * R, LANES), F32)
    static_scatter = (chunk * 3 // 4) // GROUP * GROUP
    return pl.pallas_call(
        functools.partial(_moe_kernel, chunk=chunk, static_scatter=static_scatter),
        out_shape=jax.ShapeDtypeStruct((T * R, LANES), F32),
        grid_spec=pltpu.PrefetchScalarGridSpec(
            num_scalar_prefetch=2,
            grid=(n_tiles, N_EXPERTS // E),
            in_specs=[pl.BlockSpec(memory_space=pl.ANY),
                      pl.BlockSpec(memory_space=pl.ANY),
                      tile_spec,
                      wspec(wts['w_eg']), wspec(wts['w_eu']), wspec(wts['w_ed'])],
            out_specs=tile_spec,
            scratch_shapes=[pltpu.SMEM((L,), jnp.int32),
                            pltpu.SMEM((L,), F32)]
                           + [row_buf] * (4 * E)
                           + [pltpu.SemaphoreType.DMA((2,))]),
        compiler_params=pltpu.CompilerParams(dimension_semantics=("arbitrary", "arbitrary"),
                                             vmem_limit_bytes=VMEM_LIMIT),
        name="moe",
    )(cnt, start, tok_sorted, w_sorted, hn_tiles, wts['w_eg'], wts['w_eu'], wts['w_ed'])


def _dispatch(idx, w, tmoe, chunk):
    T = idx.shape[1]
    n_tiles = T // tmoe
    per_tile = lambda a: a.reshape(TOP_K, n_tiles, tmoe).transpose(1, 0, 2).reshape(n_tiles, TOP_K * tmoe)
    e_t, w_t = per_tile(idx), per_tile(w)
    tok_l = jnp.tile(jnp.arange(tmoe, dtype=jnp.int32), TOP_K)[None, :]
    key_sorted, w_sorted = lax.sort((e_t * tmoe + tok_l, w_t), dimension=1, num_keys=1)
    tok_sorted = (key_sorted % tmoe) * ROWS_PER_TOKEN
    onehot = e_t[..., None] == jnp.arange(N_EXPERTS, dtype=jnp.int32)
    cnt = jnp.sum(onehot.astype(jnp.int32), axis=1)
    start = jnp.cumsum(cnt, axis=1) - cnt
    pad_list = lambda a: jnp.pad(a, ((0, 0), (0, chunk)))
    pad_step = lambda a: jnp.pad(a.reshape(-1), (0, EXPERTS_PER_STEP))
    return pad_step(cnt), pad_step(start), pad_list(tok_sorted), pad_list(w_sorted)


def _ple_kernel(base_ref, routed_ref, p_ref, gple_ref, wg_ref, wp_ref, y_ref):
    tm = base_ref.shape[0]
    routed = jnp.concatenate([routed_ref[pl.ds(r, tm, stride=ROWS_PER_TOKEN), :]
                              for r in range(ROWS_PER_TOKEN)], axis=1)
    x3 = base_ref[...] + routed
    hn = (x3 * _rms(x3, D_MODEL) * gple_ref[...]).astype(BF16)
    gate = jax.nn.sigmoid(_dot(hn, wg_ref[...]))
    y_ref[...] = x3 + gate * _dot(p_ref[...].astype(BF16), wp_ref[...])


def _ple(base, routed, p2d, wts, tm):
    T = base.shape[0]
    full = lambda a: pl.BlockSpec(a.shape, lambda i: (0,) * a.ndim)
    tok = lambda w: pl.BlockSpec((tm, w), lambda i: (i, 0))
    ws = [wts['g_ple'], wts['w_pg'], wts['w_pp']]
    return pl.pallas_call(
        _ple_kernel,
        out_shape=jax.ShapeDtypeStruct((T, D_MODEL), F32),
        grid=(T // tm,),
        in_specs=[tok(D_MODEL), pl.BlockSpec((tm * ROWS_PER_TOKEN, LANES), lambda i: (i, 0)),
                  tok(PLE_DIM)] + [full(w) for w in ws],
        out_specs=tok(D_MODEL),
        compiler_params=pltpu.CompilerParams(dimension_semantics=("arbitrary",),
                                             vmem_limit_bytes=VMEM_LIMIT),
        name="ple",
    )(base, routed, p2d, *ws)


def _head_pad_cols(w, width):
    K = w.shape[0]
    w = w.reshape(K, N_HEADS, width)
    return jnp.pad(w, ((0, 0), (0, 0), (0, HEAD_PAD - width))).reshape(K, N_HEADS * HEAD_PAD)


def _prep_weights(g_mix, w_in, b_gate, g_qlat, w_q_up, g_kvlat, w_kv_up, g_qn, g_kn, w_a, w_f,
                  w_o, g_ffn, w_router, b_router, w_e_gate, w_e_up, w_e_down, w_s_gate, w_s_up,
                  w_s_down, g_ple, w_ple_gate, w_ple_proj):
    half = QK_ROPE // 2
    off_kv, off_kr = Q_LORA, Q_LORA + KV_LORA
    off_f = off_kr + QK_ROPE
    off_g = off_f + FNET_W
    w1 = jnp.concatenate([
        w_in[:, :off_kr],
        jnp.pad(w_in[:, off_kr:off_f], ((0, 0), (0, LANES - QK_ROPE))),
        w_in[:, off_f:]], axis=1).astype(BF16)
    wq = _head_pad_cols(w_q_up, QK_HEAD).astype(BF16)
    wq3 = w_q_up.reshape(Q_LORA, N_HEADS, QK_HEAD)
    wq3r = jnp.concatenate([jnp.zeros_like(wq3[..., :QK_NOPE]), -wq3[..., QK_NOPE + half:],
                            wq3[..., QK_NOPE:QK_NOPE + half]], axis=-1)
    wqr = _head_pad_cols(wq3r.reshape(Q_LORA, -1), QK_HEAD).astype(BF16)
    wkv3 = w_kv_up.reshape(KV_LORA, N_HEADS, QK_NOPE + V_HEAD)
    wk = _head_pad_cols(wkv3[:, :, :QK_NOPE].reshape(KV_LORA, -1), QK_NOPE).astype(BF16)
    wv = _head_pad_cols(wkv3[:, :, QK_NOPE:].reshape(KV_LORA, -1), V_HEAD).astype(BF16)
    vone = jnp.tile(jnp.concatenate([jnp.zeros((V_HEAD,), F32), jnp.ones((HEAD_PAD - V_HEAD,), F32)]),
                    N_HEADS).reshape(1, N_HEADS * HEAD_PAD)

    def gain_tabs(g):
        gp = jnp.concatenate([jnp.zeros((QK_NOPE,), F32), g[QK_NOPE + half:], g[QK_NOPE:QK_NOPE + half]])
        pad = lambda a: jnp.pad(a, (0, HEAD_PAD - QK_HEAD)).reshape(1, HEAD_PAD)
        return pad(g), pad(gp)

    gqc, gqs = gain_tabs(g_qn)
    gkc, gks = gain_tabs(g_kn)
    j = jnp.arange(FNET_GROUP_W, dtype=jnp.int32)
    ang = (2.0 * np.pi / FNET_GROUP_W) * ((j[:, None] * j[None, :]) % FNET_GROUP_W).astype(F32)
    cs128 = jnp.concatenate([jnp.cos(ang), jnp.sin(ang)], axis=1).astype(BF16)
    w_rt = w_router.T.astype(F32)
    w_rt_hi = w_rt.astype(BF16)
    w_rt_lo = (w_rt - w_rt_hi.astype(F32)).astype(BF16)
    w_rt3 = jnp.concatenate([w_rt_hi, w_rt_lo, w_rt_hi], axis=1)
    row = lambda a: a.reshape(1, -1).astype(F32)
    return {
        'g_mix': row(g_mix), 'w1': w1, 'b_gate': row(b_gate), 'g_qlat': row(g_qlat),
        'g_kvlat': row(g_kvlat), 'wq': wq, 'wqr': wqr, 'wk': wk, 'wv': wv, 'vone': vone,
        'gqc': gqc, 'gqs': gqs, 'gkc': gkc, 'gks': gks, 'cs128': cs128,
        'w_a': w_a.astype(BF16), 'w_f': w_f.astype(BF16), 'w_o': w_o.astype(BF16),
        'g_ffn': row(g_ffn),
        'w_sgu': jnp.concatenate([w_s_gate, w_s_up], axis=1).astype(BF16),
        'w_sd': w_s_down.astype(BF16),
        'w_rt': w_rt3, 'b_r': b_router.reshape(N_EXPERTS, 1).astype(F32),
        'w_eg': w_e_gate.astype(BF16), 'w_eu': w_e_up.astype(BF16), 'w_ed': w_e_down.astype(BF16),
        'g_ple': row(g_ple), 'w_pg': w_ple_gate.astype(BF16), 'w_pp': w_ple_proj.astype(BF16),
    }


def _tables(S):
    half = QK_ROPE // 2
    freqs = 1.0 / (ROPE_THETA ** (jnp.arange(half, dtype=F32) / half))
    ang = jnp.arange(S, dtype=F32)[:, None] * freqs[None, :]
    cos, sin = jnp.cos(ang), jnp.sin(ang)
    cos_t = jnp.concatenate([jnp.ones((S, QK_NOPE), F32), cos, cos,
                             jnp.zeros((S, HEAD_PAD - QK_HEAD), F32)], axis=1)
    sin_t = jnp.concatenate([jnp.zeros((S, QK_NOPE), F32), sin, sin,
                             jnp.zeros((S, HEAD_PAD - QK_HEAD), F32)], axis=1)
    split = 64
    j = jnp.arange(S, dtype=jnp.int32)[:, None]
    ka = jnp.arange(S // split, dtype=jnp.int32)[None, :] * split
    kb = jnp.arange(split, dtype=jnp.int32)[None, :]
    ang_a = (2.0 * np.pi / S) * ((j * ka) % S).astype(F32)
    ang_b = (2.0 * np.pi / S) * ((j * kb) % S).astype(F32)
    ca, sa = jnp.cos(ang_a)[:, :, None], jnp.sin(ang_a)[:, :, None]
    cb, sb = jnp.cos(ang_b)[:, None, :], jnp.sin(ang_b)[:, None, :]
    norm = (S * FNET_GROUP_W) ** -0.5
    dft_c = ((ca * cb - sa * sb) * norm).reshape(S, S).astype(BF16)
    dft_s = ((sa * cb + ca * sb) * -norm).reshape(S, S).astype(BF16)
    return {'cos': cos_t, 'sin': sin_t, 'dft_c': dft_c, 'dft_s': dft_s}


def _block(x, p, wts, tabs, *, tm, tq, tk, tdft, tmoe, chunk):
    B, S, D = x.shape
    T = B * S
    x2d = x.reshape(T, D)
    q, k, v, fcs, gates = _inproj(x2d, S, tabs, wts, tm)
    o = _attention(q.reshape(B, S, -1), k.reshape(B, S, -1), v.reshape(B, S, -1), tq, tk)
    bf = _seqdft(fcs.reshape(B, S, -1), tabs['dft_c'], tabs['dft_s'], tdft)
    hn, base, idx, w = _merge(x2d, o.reshape(T, -1), bf.reshape(T, -1), gates, wts, tm)
    tmoe = min(tmoe, T)
    cnt, start, tok_sorted, w_sorted = _dispatch(idx, w, tmoe, chunk)
    routed = _moe(hn, cnt, start, tok_sorted, w_sorted, wts, tmoe, chunk)
    y = _ple(base, routed, p.reshape(T, -1), wts, tm)
    return y.reshape(B, S, D)


def kernel(x_prompt, x_sample, p_prompt, p_sample, g_mix, w_in, b_gate, g_qlat, w_q_up, g_kvlat, w_kv_up, g_qn, g_kn, w_a, w_f, w_o, g_ffn, w_router, b_router, w_e_gate, w_e_up, w_e_down, w_s_gate, w_s_up, w_s_down, g_ple, w_ple_gate, w_ple_proj):
    params = (g_mix, w_in, b_gate, g_qlat, w_q_up, g_kvlat, w_kv_up, g_qn, g_kn, w_a, w_f, w_o,
              g_ffn, w_router, b_router, w_e_gate, w_e_up, w_e_down, w_s_gate, w_s_up, w_s_down,
              g_ple, w_ple_gate, w_ple_proj)
    depth = g_mix.shape[0]
    cfg = dict(tm=512, tq=512, tk=1024, tdft=512, tmoe=2048, chunk=288)
    xp, xs = x_prompt, x_sample
    tabs = _tables(x_prompt.shape[1])
    tabs_s = tabs if x_sample.shape[1] == x_prompt.shape[1] else _tables(x_sample.shape[1])
    for l in range(depth):
        wts = _prep_weights(*[a[l] for a in params])
        xp = _block(xp, p_prompt[l], wts, tabs, **cfg)
        xs = _block(xs, p_sample[l], wts, tabs_s, **cfg)
    return (xp, xs)
```

```python
import functools

import jax
import jax.numpy as jnp
import numpy as np
from jax import lax
from jax.experimental import pallas as pl
from jax.experimental.pallas import tpu as pltpu

D_MODEL = 1024
N_HEADS = 8
QK_NOPE = 64
QK_ROPE = 32
QK_HEAD = QK_NOPE + QK_ROPE
V_HEAD = 64
Q_LORA = 384
KV_LORA = 256
ROPE_THETA = 10000.0
FNET_GROUPS = 4
FNET_GROUP_W = 128
FNET_W = FNET_GROUPS * FNET_GROUP_W
N_EXPERTS = 64
TOP_K = 8
N_EXPERT_GROUPS = 8
TOPK_GROUPS = 4
EXPERT_FF = 256
SHARED_FF = 256
ROUTED_SCALE = 2.5
PLE_DIM = 256
EPS = 1e-6

LANES = 128
SUBLANES = 8
HEAD_PAD = LANES
ROWS_PER_TOKEN = D_MODEL // LANES
GROUP = 8
DOWN_SPLIT = 2
EXPERTS_PER_STEP = 2
HEADS_PER_STEP = 4

C_Q = 0
C_KV = C_Q + Q_LORA
C_KR = C_KV + KV_LORA
C_F = C_KR + LANES
C_G = C_F + FNET_W
C_END = C_G + 2 * D_MODEL

VMEM_LIMIT = 56 * 1024 * 1024
BF16 = jnp.bfloat16
F32 = jnp.float32


def _rms(x, n):
    return lax.rsqrt(jnp.sum(x * x, axis=-1, keepdims=True) * (1.0 / n) + EPS)


def _dot(a, b):
    return jnp.dot(a, b, preferred_element_type=F32)


def _rot_partner(x, lane):
    half = QK_ROPE // 2
    return jnp.where(lane < QK_NOPE + half, -pltpu.roll(x, HEAD_PAD - half, 1), pltpu.roll(x, half, 1))


def _inproj_kernel(x_ref, cos_ref, sin_ref, gmix_ref, w1_ref, bg_ref, gq_ref, gkv_ref,
                   wq_ref, wqr_ref, wk_ref, wv_ref, vone_ref, gqc_ref, gqs_ref, gkc_ref,
                   gks_ref, cs_ref, q_out, k_out, v_out, f_out, g_out):
    x = x_ref[...]
    tm = x.shape[0]
    h = (x * _rms(x, D_MODEL) * gmix_ref[...]).astype(BF16)
    proj = _dot(h, w1_ref[...])
    q_lat = proj[:, C_Q:C_KV]
    q_lat = (q_lat * _rms(q_lat, Q_LORA) * gq_ref[...]).astype(BF16)
    kv_lat = proj[:, C_KV:C_KR]
    kv_lat = (kv_lat * _rms(kv_lat, KV_LORA) * gkv_ref[...]).astype(BF16)
    kr = proj[:, C_KR:C_F]
    f_in = proj[:, C_F:C_G].astype(BF16)
    g_out[...] = jax.nn.sigmoid(proj[:, C_G:C_END] + bg_ref[...]).astype(BF16)
    for grp in range(FNET_GROUPS):
        gs = slice(grp * FNET_GROUP_W, (grp + 1) * FNET_GROUP_W)
        fcs = _dot(f_in[:, gs], cs_ref[...]).astype(BF16)
        f_out[:, gs] = fcs[:, :FNET_GROUP_W]
        f_out[:, FNET_W + grp * FNET_GROUP_W:FNET_W + (grp + 1) * FNET_GROUP_W] = fcs[:, FNET_GROUP_W:]
    v_out[...] = (_dot(kv_lat, wv_ref[...]) + vone_ref[...]).astype(BF16)

    q = _dot(q_lat, wq_ref[...])
    qr = _dot(q_lat, wqr_ref[...])
    k_nope = _dot(kv_lat, wk_ref[...])
    lane = lax.broadcasted_iota(jnp.int32, (tm, HEAD_PAD), 1)
    kr_placed = pltpu.roll(kr, QK_NOPE, 1)
    kr_partner = _rot_partner(kr_placed, lane)
    cos = cos_ref[...]
    sin = sin_ref[...]
    qc = gqc_ref[...] * cos
    qs = gqs_ref[...] * sin
    kc = gkc_ref[...] * cos
    ks = gks_ref[...] * sin
    scale = QK_HEAD ** -0.5 * np.log2(np.e)
    for hd in range(N_HEADS):
        sl = slice(hd * HEAD_PAD, (hd + 1) * HEAD_PAD)
        qh = q[:, sl]
        rq = _rms(qh, QK_HEAD) * scale
        q_out[:, sl] = ((qh * qc + qr[:, sl] * qs) * rq).astype(BF16)
        kh = k_nope[:, sl] + kr_placed
        rk = _rms(kh, QK_HEAD)
        k_out[:, sl] = ((kh * kc + kr_partner * ks) * rk).astype(BF16)


def _inproj(x2d, S, tabs, wts, tm):
    T = x2d.shape[0]
    n_pos = S // tm
    full = lambda a: pl.BlockSpec(a.shape, lambda i: (0,) * a.ndim)
    tok = lambda w: pl.BlockSpec((tm, w), lambda i: (i, 0))
    pos = pl.BlockSpec((tm, HEAD_PAD), lambda i: (i % n_pos, 0))
    names = ['g_mix', 'w1', 'b_gate', 'g_qlat', 'g_kvlat', 'wq', 'wqr', 'wk', 'wv', 'vone',
             'gqc', 'gqs', 'gkc', 'gks', 'cs128']
    ws = [wts[n] for n in names]
    return pl.pallas_call(
        _inproj_kernel,
        out_shape=(jax.ShapeDtypeStruct((T, N_HEADS * HEAD_PAD), BF16),
                   jax.ShapeDtypeStruct((T, N_HEADS * HEAD_PAD), BF16),
                   jax.ShapeDtypeStruct((T, N_HEADS * HEAD_PAD), BF16),
                   jax.ShapeDtypeStruct((T, 2 * FNET_W), BF16),
                   jax.ShapeDtypeStruct((T, 2 * D_MODEL), BF16)),
        grid=(T // tm,),
        in_specs=[tok(D_MODEL), pos, pos] + [full(w) for w in ws],
        out_specs=(tok(N_HEADS * HEAD_PAD), tok(N_HEADS * HEAD_PAD), tok(N_HEADS * HEAD_PAD),
                   tok(2 * FNET_W), tok(2 * D_MODEL)),
        compiler_params=pltpu.CompilerParams(dimension_semantics=("arbitrary",),
                                             vmem_limit_bytes=VMEM_LIMIT),
        name="inproj",
    )(x2d, tabs['cos'], tabs['sin'], *ws)


def _attn_kernel(q_ref, k_ref, v_ref, o_ref, s_a, s_b, *, tk):
    S = k_ref.shape[0]
    tq = q_ref.shape[0]
    nblk = tk // LANES
    n_chunks = S // tk
    heads = [slice(a * HEAD_PAD, (a + 1) * HEAD_PAD) for a in range(HEADS_PER_STEP)]
    s_bufs = (s_a, s_b)

    def scores(c, s_ref):
        for a, hs in enumerate(heads):
            s_ref[a] = lax.dot_general(q_ref[:, hs], k_ref[c * tk:(c + 1) * tk, hs],
                                       (((1,), (1,)), ((), ())), preferred_element_type=F32)

    def consume(c, s_ref, carry):
        new = []
        for a, hs in enumerate(heads):
            m, acc = carry[a]
            mx = s_ref[a, :, :LANES]
            for j in range(1, nblk):
                mx = jnp.maximum(mx, s_ref[a, :, j * LANES:(j + 1) * LANES])
            m_new = jnp.maximum(m, jnp.max(mx, axis=-1, keepdims=True))
            alpha = jnp.exp2(m - m_new)
            p = jnp.concatenate(
                [jnp.exp2((s_ref[a, :, j * LANES:(j + 1) * LANES] - m_new).astype(BF16))
                 for j in range(nblk)], axis=1)
            acc = alpha * acc + _dot(p, v_ref[c * tk:(c + 1) * tk, hs])
            new.append((m_new, acc))
        return new

    carry = [(jnp.full((tq, LANES), -jnp.inf, F32), jnp.zeros((tq, HEAD_PAD), F32))
             for _ in heads]
    scores(0, s_bufs[0])
    for c in range(n_chunks):
        if c + 1 < n_chunks:
            scores(c + 1, s_bufs[(c + 1) % 2])
        carry = consume(c, s_bufs[c % 2], carry)
    lane = lax.broadcasted_iota(jnp.int32, (tq, 2 * V_HEAD), 1)
    for pair in range(HEADS_PER_STEP // 2):
        acc_a, acc_b = carry[2 * pair][1], carry[2 * pair + 1][1]
        o_a = acc_a / pltpu.roll(acc_a, V_HEAD, 1)
        o_b = acc_b / pltpu.roll(acc_b, V_HEAD, 1)
        o_ref[:, pair * 2 * V_HEAD:(pair + 1) * 2 * V_HEAD] = jnp.where(
            lane < V_HEAD, o_a, pltpu.roll(o_b, V_HEAD, 1)).astype(o_ref.dtype)


def _attention(q, k, v, tq, tk):
    B, S, _ = q.shape
    hp = HEADS_PER_STEP
    return pl.pallas_call(
        functools.partial(_attn_kernel, tk=tk),
        out_shape=jax.ShapeDtypeStruct((B, S, N_HEADS * V_HEAD), BF16),
        grid=(B, N_HEADS // hp, S // tq),
        in_specs=[pl.BlockSpec((None, tq, hp * HEAD_PAD), lambda b, p, i: (b, i, p)),
                  pl.BlockSpec((None, S, hp * HEAD_PAD), lambda b, p, i: (b, 0, p)),
                  pl.BlockSpec((None, S, hp * HEAD_PAD), lambda b, p, i: (b, 0, p))],
        out_specs=pl.BlockSpec((None, tq, hp * V_HEAD), lambda b, p, i: (b, i, p)),
        scratch_shapes=[pltpu.VMEM((hp, tq, tk), F32), pltpu.VMEM((hp, tq, tk), F32)],
        compiler_params=pltpu.CompilerParams(
            dimension_semantics=("arbitrary", "arbitrary", "arbitrary"),
            vmem_limit_bytes=VMEM_LIMIT),
        name="attention",
    )(q, k, v)


def _seqdft_kernel(c_ref, s_ref, f_ref, o_ref):
    y = _dot(c_ref[...], f_ref[:, :FNET_W]) + _dot(s_ref[...], f_ref[:, FNET_W:])
    o_ref[...] = y.astype(o_ref.dtype)


def _seqdft(fcs, ctab, stab, tm):
    B, S, _ = fcs.shape
    return pl.pallas_call(
        _seqdft_kernel,
        out_shape=jax.ShapeDtypeStruct((B, S, FNET_W), BF16),
        grid=(S // tm, B),
        in_specs=[pl.BlockSpec((tm, S), lambda i, b: (i, 0)),
                  pl.BlockSpec((tm, S), lambda i, b: (i, 0)),
                  pl.BlockSpec((None, S, 2 * FNET_W), lambda i, b: (b, 0, 0))],
        out_specs=pl.BlockSpec((None, tm, FNET_W), lambda i, b: (b, i, 0)),
        compiler_params=pltpu.CompilerParams(dimension_semantics=("arbitrary", "arbitrary"),
                                             vmem_limit_bytes=VMEM_LIMIT),
        name="seqdft",
    )(ctab, stab, fcs)


def _first_argmax(v, iota, n):
    m = jnp.max(v, axis=0, keepdims=True)
    i = jnp.min(jnp.where(v == m, iota, n), axis=0, keepdims=True)
    return m, i


def _route(logits_t, bias):
    tm = logits_t.shape[1]
    per = N_EXPERTS // N_EXPERT_GROUPS
    scores = jax.nn.sigmoid(logits_t)
    sel = scores + bias
    neg = jnp.float32(-jnp.inf)
    iota_p = lax.broadcasted_iota(jnp.int32, (per, tm), 0)
    iota_g = lax.broadcasted_iota(jnp.int32, (N_EXPERT_GROUPS, tm), 0)
    gscore = jnp.zeros((N_EXPERT_GROUPS, tm), F32)
    for g in range(N_EXPERT_GROUPS):
        v = sel[g * per:(g + 1) * per, :]
        m1, i1 = _first_argmax(v, iota_p, per)
        m2 = jnp.max(jnp.where(iota_p == i1, neg, v), axis=0, keepdims=True)
        gscore = jnp.where(iota_g == g, m1 + m2, gscore)
    gmask = jnp.zeros((N_EXPERT_GROUPS, tm), jnp.bool_)
    v = gscore
    for _ in range(TOPK_GROUPS):
        _, i = _first_argmax(v, iota_g, N_EXPERT_GROUPS)
        hit = iota_g == i
        gmask = jnp.logical_or(gmask, hit)
        v = jnp.where(hit, neg, v)
    gkeep = jnp.where(gmask, 1.0, 0.0)
    parts = []
    for g in range(N_EXPERT_GROUPS):
        keep = gkeep[g:g + 1, :] > 0.5
        parts.append(jnp.where(keep, sel[g * per:(g + 1) * per, :], neg))
    v = jnp.concatenate(parts, axis=0)
    iota_e = lax.broadcasted_iota(jnp.int32, (N_EXPERTS, tm), 0)
    iota_k = lax.broadcasted_iota(jnp.int32, (TOP_K, tm), 0)
    idx = jnp.zeros((TOP_K, tm), jnp.int32)
    wts = jnp.zeros((TOP_K, tm), F32)
    for kk in range(TOP_K):
        _, i = _first_argmax(v, iota_e, N_EXPERTS)
        hit = iota_e == i
        wk = jnp.sum(jnp.where(hit, scores, 0.0), axis=0, keepdims=True)
        v = jnp.where(hit, neg, v)
        idx = jnp.where(iota_k == kk, i, idx)
        wts = jnp.where(iota_k == kk, wk, wts)
    wts = wts / jnp.sum(wts, axis=0, keepdims=True) * ROUTED_SCALE
    return idx, wts


def _merge_kernel(x_ref, o_ref, bf_ref, g_ref, wa_ref, wf_ref, wo_ref, gffn_ref, wsgu_ref,
                  wsd_ref, wrt_ref, br_ref, hn_out, base_out, idx_out, w_out):
    a = _dot(o_ref[...], wa_ref[...])
    b = _dot(bf_ref[...], wf_ref[...])
    g = g_ref[...].astype(F32)
    m = (g[:, :D_MODEL] * a + g[:, D_MODEL:] * b).astype(BF16)
    x2 = x_ref[...] + _dot(m, wo_ref[...])
    hn = x2 * _rms(x2, D_MODEL) * gffn_ref[...]
    tm = hn.shape[0]
    for r in range(ROWS_PER_TOKEN):
        hn_out[pl.ds(r, tm, stride=ROWS_PER_TOKEN), :] = hn[:, r * LANES:(r + 1) * LANES]
    hb = hn.astype(BF16)
    gu = _dot(hb, wsgu_ref[...])
    act = (jax.nn.silu(gu[:, :SHARED_FF]) * gu[:, SHARED_FF:]).astype(BF16)
    base_out[...] = x2 + _dot(act, wsd_ref[...])
    h_lo = (hn - hb.astype(F32)).astype(BF16)
    h3 = jnp.concatenate([hb, hb, h_lo], axis=1)
    logits_t = lax.dot_general(wrt_ref[...], h3, (((1,), (1,)), ((), ())),
                               preferred_element_type=F32)
    idx, wts = _route(logits_t, br_ref[...])
    idx_out[...] = idx
    w_out[...] = wts


def _merge(x2d, o2d, bf2d, gates, wts, tm):
    T = x2d.shape[0]
    full = lambda a: pl.BlockSpec(a.shape, lambda i: (0,) * a.ndim)
    tok = lambda w: pl.BlockSpec((tm, w), lambda i: (i, 0))
    names = ['w_a', 'w_f', 'w_o', 'g_ffn', 'w_sgu', 'w_sd', 'w_rt', 'b_r']
    ws = [wts[n] for n in names]
    return pl.pallas_call(
        _merge_kernel,
        out_shape=(jax.ShapeDtypeStruct((T * ROWS_PER_TOKEN, LANES), F32),
                   jax.ShapeDtypeStruct((T, D_MODEL), F32),
                   jax.ShapeDtypeStruct((TOP_K, T), jnp.int32),
                   jax.ShapeDtypeStruct((TOP_K, T), F32)),
        grid=(T // tm,),
        in_specs=[tok(D_MODEL), tok(N_HEADS * V_HEAD), tok(FNET_W), tok(2 * D_MODEL)]
                 + [full(w) for w in ws],
        out_specs=(pl.BlockSpec((tm * ROWS_PER_TOKEN, LANES), lambda i: (i, 0)), tok(D_MODEL),
                   pl.BlockSpec((TOP_K, tm), lambda i: (0, i)),
                   pl.BlockSpec((TOP_K, tm), lambda i: (0, i))),
        compiler_params=pltpu.CompilerParams(dimension_semantics=("arbitrary",),
                                             vmem_limit_bytes=VMEM_LIMIT),
        name="merge",
    )(x2d, o2d, bf2d, gates, *ws)


def _moe_kernel(cnt_ref, start_ref, tok_hbm, w_hbm, hn_ref, wg_ref, wu_ref, wd_ref, out_ref,
                tok_s, w_s, *scratch, chunk, static_scatter):
    E = EXPERTS_PER_STEP
    g_bufs = (scratch[0:E], scratch[E:2 * E])
    y_bufs = (scratch[2 * E:3 * E], scratch[3 * E:4 * E])
    sem = scratch[4 * E]
    i = pl.program_id(0)
    p = pl.program_id(1)
    n_groups = N_EXPERTS // E
    R = ROWS_PER_TOKEN
    first = i * N_EXPERTS + p * E

    def tile_at(row):
        return pl.ds(pl.multiple_of(row * R, R), R)

    def token_tile(off):
        return pl.ds(pl.multiple_of(off, R), R)

    def gather_group(base, j0, gbuf):
        vals = [hn_ref[token_tile(tok_s[base + j0 + r]), :] for r in range(GROUP)]
        for r in range(GROUP):
            gbuf[tile_at(j0 + r), :] = vals[r]

    def scatter_group(base, j0, ybuf):
        dsts = [token_tile(tok_s[base + j0 + r]) for r in range(GROUP)]
        vals = [out_ref[dsts[r], :] + w_s[base + j0 + r] * ybuf[tile_at(j0 + r), :]
                for r in range(GROUP)]
        for r in range(GROUP):
            out_ref[dsts[r], :] = vals[r]

    def gather_rolled(base, rows, gbuf):
        def group(g, _):
            gather_group(base, g * GROUP, gbuf)
            return 0

        def row(j, _):
            gbuf[tile_at(j), :] = hn_ref[token_tile(tok_s[base + j]), :]
            return 0

        lax.fori_loop(0, rows // GROUP, group, 0)
        lax.fori_loop(rows // GROUP * GROUP, rows, row, 0)

    def scatter_rolled(base, first_row, rows, ybuf):
        def group(g, _):
            scatter_group(base, g * GROUP, ybuf)
            return 0

        def row(j, _):
            dst = token_tile(tok_s[base + j])
            out_ref[dst, :] = out_ref[dst, :] + w_s[base + j] * ybuf[tile_at(j), :]
            return 0

        lax.fori_loop(first_row // GROUP, rows // GROUP, group, 0)
        lax.fori_loop(jnp.maximum(rows // GROUP * GROUP, first_row), rows, row, 0)

    def matmul_rows(gbuf):
        return jnp.concatenate([gbuf[pl.ds(r, chunk, stride=R), :] for r in range(R)],
                               axis=1).astype(BF16)

    def ffn_out(k, act, ybuf, col0, ncols):
        y = _dot(act, wd_ref[k, :, col0:col0 + ncols])
        for r in range(ncols // LANES):
            ybuf[pl.ds(col0 // LANES + r, chunk, stride=R), :] = y[:, r * LANES:(r + 1) * LANES]

    def ffn(k, gbuf, ybuf):
        xg = matmul_rows(gbuf)
        act = (jax.nn.silu(_dot(xg, wg_ref[k])) * _dot(xg, wu_ref[k])).astype(BF16)
        ffn_out(k, act, ybuf, 0, D_MODEL)

    ks = range(E)
    n = [cnt_ref[first + k] for k in ks]
    st = [start_ref[first + k] for k in ks]
    rows = [jnp.minimum(n[k], chunk) for k in ks]
    st_next = [start_ref[first + E + k] for k in ks]
    prev = [jnp.maximum(first - E + k, 0) for k in ks]
    st_prev = [start_ref[prev[k]] for k in ks]
    rows_prev = [jnp.where(p > 0, jnp.minimum(cnt_ref[prev[k]], chunk), 0) for k in ks]
    min_rows_prev = functools.reduce(jnp.minimum, rows_prev)

    @pl.when(jnp.logical_and(i == 0, p == 0))
    def _():
        for buf in g_bufs[0] + g_bufs[1]:
            buf[...] = jnp.zeros_like(buf)

    @pl.when(p == 0)
    def _():
        cp_t = pltpu.make_async_copy(tok_hbm.at[i], tok_s, sem.at[0])
        cp_w = pltpu.make_async_copy(w_hbm.at[i], w_s, sem.at[1])
        cp_t.start()
        cp_w.start()
        out_ref[...] = jnp.zeros_like(out_ref)
        cp_t.wait()
        cp_w.wait()
        for k in ks:
            gather_rolled(st[k], rows[k], g_bufs[0][k])

    def pipelined_step(parity):
        g_cur, g_next = g_bufs[parity], g_bufs[1 - parity]
        y_cur, y_prev = y_bufs[parity], y_bufs[1 - parity]

        @pl.when(min_rows_prev >= static_scatter)
        def _():
            groups = []
            for k in ks:
                sc = [functools.partial(scatter_group, st_prev[k], j0, y_prev[k])
                      for j0 in range(0, static_scatter, GROUP)]
                ga = [functools.partial(gather_group, st_next[k], j0, g_next[k])
                      for j0 in range(0, chunk, GROUP)]
                while sc or ga:
                    if sc:
                        groups.append(sc.pop(0))
                    if ga:
                        groups.append(ga.pop(0))
            n_stages = (2 + DOWN_SPLIT) * E
            per_stage = -(-len(groups) // n_stages)

            def emit_groups():
                for _ in range(min(per_stage, len(groups))):
                    groups.pop(0)()

            acts = []
            for k in ks:
                xg = matmul_rows(g_cur[k])
                g = _dot(xg, wg_ref[k])
                emit_groups()
                u = _dot(xg, wu_ref[k])
                emit_groups()
                acts.append((jax.nn.silu(g) * u).astype(BF16))
            ncols = D_MODEL // DOWN_SPLIT
            for k in ks:
                for c in range(DOWN_SPLIT):
                    ffn_out(k, acts[k], y_cur[k], c * ncols, ncols)
                    emit_groups()
            assert not groups
            for k in ks:
                scatter_rolled(st_prev[k], static_scatter, rows_prev[k], y_prev[k])

        @pl.when(min_rows_prev < static_scatter)
        def _():
            for k in ks:
                gather_rolled(st_next[k], chunk, g_next[k])
                ffn(k, g_cur[k], y_cur[k])
                scatter_rolled(st_prev[k], 0, rows_prev[k], y_prev[k])

        @pl.when(p == n_groups - 1)
        def _():
            for k in ks:
                scatter_rolled(st[k], 0, rows[k], y_cur[k])

        for k in ks:
            def extra_chunk(c, _, k=k):
                base = st[k] + c * chunk
                rows_c = jnp.minimum(chunk, n[k] - c * chunk)
                gather_rolled(base, rows_c, g_cur[0])
                ffn(k, g_cur[0], y_prev[0])
                scatter_rolled(base, 0, rows_c, y_prev[0])
                return 0

            lax.fori_loop(1, pl.cdiv(n[k], chunk), extra_chunk, 0)

    @pl.when(p % 2 == 0)
    def _():
        pipelined_step(0)

    @pl.when(p % 2 == 1)
    def _():
        pipelined_step(1)


def _moe(hn_tiles, cnt, start, tok_sorted, w_sorted, wts, tmoe, chunk):
    R = ROWS_PER_TOKEN
    T = hn_tiles.shape[0] // R
    n_tiles = T // tmoe
    L = tok_sorted.shape[1]
    E = EXPERTS_PER_STEP
    wspec = lambda a: pl.BlockSpec((E,) + a.shape[1:], lambda i, p, *_: (p, 0, 0))
    tile_spec = pl.BlockSpec((tmoe * R, LANES), lambda i, p, *_: (i, 0))
    row_buf = pltpu.VMEM((chunk * R, LANES), F32)
    static_scatter = (chunk * 3 // 4) // GROUP * GROUP
    return pl.pallas_call(
        functools.partial(_moe_kernel, chunk=chunk, static_scatter=static_scatter),
        out_shape=jax.ShapeDtypeStruct((T * R, LANES), F32),
        grid_spec=pltpu.PrefetchScalarGridSpec(
            num_scalar_prefetch=2,
            grid=(n_tiles, N_EXPERTS // E),
            in_specs=[pl.BlockSpec(memory_space=pl.ANY),
                      pl.BlockSpec(memory_space=pl.ANY),
                      tile_spec,
                      wspec(wts['w_eg']), wspec(wts['w_eu']), wspec(wts['w_ed'])],
            out_specs=tile_spec,
            scratch_shapes=[pltpu.SMEM((L,), jnp.int32),
                            pltpu.SMEM((L,), F32)]
                           + [row_buf] * (4 * E)
                           + [pltpu.SemaphoreType.DMA((2,))]),
        compiler_params=pltpu.CompilerParams(dimension_semantics=("arbitrary", "arbitrary"),
                                             vmem_limit_bytes=VMEM_LIMIT),
        name="moe",
    )(cnt, start, tok_sorted, w_sorted, hn_tiles, wts['w_eg'], wts['w_eu'], wts['w_ed'])


def _dispatch(idx, w, tmoe, chunk):
    T = idx.shape[1]
    n_tiles = T // tmoe
    per_tile = lambda a: a.reshape(TOP_K, n_tiles, tmoe).transpose(1, 0, 2).reshape(n_tiles, TOP_K * tmoe)
    e_t, w_t = per_tile(idx), per_tile(w)
    tok_l = jnp.tile(jnp.arange(tmoe, dtype=jnp.int32), TOP_K)[None, :]
    key_sorted, w_sorted = lax.sort((e_t * tmoe + tok_l, w_t), dimension=1, num_keys=1)
    tok_sorted = (key_sorted % tmoe) * ROWS_PER_TOKEN
    onehot = e_t[..., None] == jnp.arange(N_EXPERTS, dtype=jnp.int32)
    cnt = jnp.sum(onehot.astype(jnp.int32), axis=1)
    start = jnp.cumsum(cnt, axis=1) - cnt
    pad_list = lambda a: jnp.pad(a, ((0, 0), (0, chunk)))
    pad_step = lambda a: jnp.pad(a.reshape(-1), (0, EXPERTS_PER_STEP))
    return pad_step(cnt), pad_step(start), pad_list(tok_sorted), pad_list(w_sorted)


def _ple_kernel(base_ref, routed_ref, p_ref, gple_ref, wg_ref, wp_ref, y_ref):
    tm = base_ref.shape[0]
    routed = jnp.concatenate([routed_ref[pl.ds(r, tm, stride=ROWS_PER_TOKEN), :]
                              for r in range(ROWS_PER_TOKEN)], axis=1)
    x3 = base_ref[...] + routed
    hn = (x3 * _rms(x3, D_MODEL) * gple_ref[...]).astype(BF16)
    gate = jax.nn.sigmoid(_dot(hn, wg_ref[...]))
    y_ref[...] = x3 + gate * _dot(p_ref[...].astype(BF16), wp_ref[...])


def _ple(base, routed, p2d, wts, tm):
    T = base.shape[0]
    full = lambda a: pl.BlockSpec(a.shape, lambda i: (0,) * a.ndim)
    tok = lambda w: pl.BlockSpec((tm, w), lambda i: (i, 0))
    ws = [wts['g_ple'], wts['w_pg'], wts['w_pp']]
    return pl.pallas_call(
        _ple_kernel,
        out_shape=jax.ShapeDtypeStruct((T, D_MODEL), F32),
        grid=(T // tm,),
        in_specs=[tok(D_MODEL), pl.BlockSpec((tm * ROWS_PER_TOKEN, LANES), lambda i: (i, 0)),
                  tok(PLE_DIM)] + [full(w) for w in ws],
        out_specs=tok(D_MODEL),
        compiler_params=pltpu.CompilerParams(dimension_semantics=("arbitrary",),
                                             vmem_limit_bytes=VMEM_LIMIT),
        name="ple",
    )(base, routed, p2d, *ws)


def _head_pad_cols(w, width):
    K = w.shape[0]
    w = w.reshape(K, N_HEADS, width)
    return jnp.pad(w, ((0, 0), (0, 0), (0, HEAD_PAD - width))).reshape(K, N_HEADS * HEAD_PAD)


def _prep_weights(g_mix, w_in, b_gate, g_qlat, w_q_up, g_kvlat, w_kv_up, g_qn, g_kn, w_a, w_f,
                  w_o, g_ffn, w_router, b_router, w_e_gate, w_e_up, w_e_down, w_s_gate, w_s_up,
                  w_s_down, g_ple, w_ple_gate, w_ple_proj):
    half = QK_ROPE // 2
    off_kv, off_kr = Q_LORA, Q_LORA + KV_LORA
    off_f = off_kr + QK_ROPE
    off_g = off_f + FNET_W
    w1 = jnp.concatenate([
        w_in[:, :off_kr],
        jnp.pad(w_in[:, off_kr:off_f], ((0, 0), (0, LANES - QK_ROPE))),
        w_in[:, off_f:]], axis=1).astype(BF16)
    wq = _head_pad_cols(w_q_up, QK_HEAD).astype(BF16)
    wq3 = w_q_up.reshape(Q_LORA, N_HEADS, QK_HEAD)
    wq3r = jnp.concatenate([jnp.zeros_like(wq3[..., :QK_NOPE]), -wq3[..., QK_NOPE + half:],
                            wq3[..., QK_NOPE:QK_NOPE + half]], axis=-1)
    wqr = _head_pad_cols(wq3r.reshape(Q_LORA, -1), QK_HEAD).astype(BF16)
    wkv3 = w_kv_up.reshape(KV_LORA, N_HEADS, QK_NOPE + V_HEAD)
    wk = _head_pad_cols(wkv3[:, :, :QK_NOPE].reshape(KV_LORA, -1), QK_NOPE).astype(BF16)
    wv = _head_pad_cols(wkv3[:, :, QK_NOPE:].reshape(KV_LORA, -1), V_HEAD).astype(BF16)
    vone = jnp.tile(jnp.concatenate([jnp.zeros((V_HEAD,), F32), jnp.ones((HEAD_PAD - V_HEAD,), F32)]),
                    N_HEADS).reshape(1, N_HEADS * HEAD_PAD)

    def gain_tabs(g):
        gp = jnp.concatenate([jnp.zeros((QK_NOPE,), F32), g[QK_NOPE + half:], g[QK_NOPE:QK_NOPE + half]])
        pad = lambda a: jnp.pad(a, (0, HEAD_PAD - QK_HEAD)).reshape(1, HEAD_PAD)
        return pad(g), pad(gp)

    gqc, gqs = gain_tabs(g_qn)
    gkc, gks = gain_tabs(g_kn)
    j = jnp.arange(FNET_GROUP_W, dtype=jnp.int32)
    ang = (2.0 * np.pi / FNET_GROUP_W) * ((j[:, None] * j[None, :]) % FNET_GROUP_W).astype(F32)
    cs128 = jnp.concatenate([jnp.cos(ang), jnp.sin(ang)], axis=1).astype(BF16)
    w_rt = w_router.T.astype(F32)
    w_rt_hi = w_rt.astype(BF16)
    w_rt_lo = (w_rt - w_rt_hi.astype(F32)).astype(BF16)
    w_rt3 = jnp.concatenate([w_rt_hi, w_rt_lo, w_rt_hi], axis=1)
    row = lambda a: a.reshape(1, -1).astype(F32)
    return {
        'g_mix': row(g_mix), 'w1': w1, 'b_gate': row(b_gate), 'g_qlat': row(g_qlat),
        'g_kvlat': row(g_kvlat), 'wq': wq, 'wqr': wqr, 'wk': wk, 'wv': wv, 'vone': vone,
        'gqc': gqc, 'gqs': gqs, 'gkc': gkc, 'gks': gks, 'cs128': cs128,
        'w_a': w_a.astype(BF16), 'w_f': w_f.astype(BF16), 'w_o': w_o.astype(BF16),
        'g_ffn': row(g_ffn),
        'w_sgu': jnp.concatenate([w_s_gate, w_s_up], axis=1).astype(BF16),
        'w_sd': w_s_down.astype(BF16),
        'w_rt': w_rt3, 'b_r': b_router.reshape(N_EXPERTS, 1).astype(F32),
        'w_eg': w_e_gate.astype(BF16), 'w_eu': w_e_up.astype(BF16), 'w_ed': w_e_down.astype(BF16),
        'g_ple': row(g_ple), 'w_pg': w_ple_gate.astype(BF16), 'w_pp': w_ple_proj.astype(BF16),
    }


def _tables(S):
    half = QK_ROPE // 2
    freqs = 1.0 / (ROPE_THETA ** (jnp.arange(half, dtype=F32) / half))
    ang = jnp.arange(S, dtype=F32)[:, None] * freqs[None, :]
    cos, sin = jnp.cos(ang), jnp.sin(ang)
    cos_t = jnp.concatenate([jnp.ones((S, QK_NOPE), F32), cos, cos,
                             jnp.zeros((S, HEAD_PAD - QK_HEAD), F32)], axis=1)
    sin_t = jnp.concatenate([jnp.zeros((S, QK_NOPE), F32), sin, sin,
                             jnp.zeros((S, HEAD_PAD - QK_HEAD), F32)], axis=1)
    split = 64
    j = jnp.arange(S, dtype=jnp.int32)[:, None]
    ka = jnp.arange(S // split, dtype=jnp.int32)[None, :] * split
    kb = jnp.arange(split, dtype=jnp.int32)[None, :]
    ang_a = (2.0 * np.pi / S) * ((j * ka) % S).astype(F32)
    ang_b = (2.0 * np.pi / S) * ((j * kb) % S).astype(F32)
    ca, sa = jnp.cos(ang_a)[:, :, None], jnp.sin(ang_a)[:, :, None]
    cb, sb = jnp.cos(ang_b)[:, None, :], jnp.sin(ang_b)[:, None, :]
    norm = (S * FNET_GROUP_W) ** -0.5
    dft_c = ((ca * cb - sa * sb) * norm).reshape(S, S).astype(BF16)
    dft_s = ((sa * cb + ca * sb) * -norm).reshape(S, S).astype(BF16)
    return {'cos': cos_t, 'sin': sin_t, 'dft_c': dft_c, 'dft_s': dft_s}


def _block(x, p, wts, tabs, *, tm, tq, tk, tdft, tmoe, chunk):
    B, S, D = x.shape
    T = B * S
    x2d = x.reshape(T, D)
    q, k, v, fcs, gates = _inproj(x2d, S, tabs, wts, tm)
    o = _attention(q.reshape(B, S, -1), k.reshape(B, S, -1), v.reshape(B, S, -1), tq, tk)
    bf = _seqdft(fcs.reshape(B, S, -1), tabs['dft_c'], tabs['dft_s'], tdft)
    hn, base, idx, w = _merge(x2d, o.reshape(T, -1), bf.reshape(T, -1), gates, wts, tm)
    tmoe = min(tmoe, T)
    cnt, start, tok_sorted, w_sorted = _dispatch(idx, w, tmoe, chunk)
    routed = _moe(hn, cnt, start, tok_sorted, w_sorted, wts, tmoe, chunk)
    y = _ple(base, routed, p.reshape(T, -1), wts, tm)
    return y.reshape(B, S, D)


def kernel(x_prompt, x_sample, p_prompt, p_sample, g_mix, w_in, b_gate, g_qlat, w_q_up, g_kvlat, w_kv_up, g_qn, g_kn, w_a, w_f, w_o, g_ffn, w_router, b_router, w_e_gate, w_e_up, w_e_down, w_s_gate, w_s_up, w_s_down, g_ple, w_ple_gate, w_ple_proj):
    params = (g_mix, w_in, b_gate, g_qlat, w_q_up, g_kvlat, w_kv_up, g_qn, g_kn, w_a, w_f, w_o,
              g_ffn, w_router, b_router, w_e_gate, w_e_up, w_e_down, w_s_gate, w_s_up, w_s_down,
              g_ple, w_ple_gate, w_ple_proj)
    depth = g_mix.shape[0]
    cfg = dict(tm=1024, tq=512, tk=1024, tdft=512, tmoe=2048, chunk=288)
    xp, xs = x_prompt, x_sample
    tabs = _tables(x_prompt.shape[1])
    tabs_s = tabs if x_sample.shape[1] == x_prompt.shape[1] else _tables(x_sample.shape[1])
    for l in range(depth):
        wts = _prep_weights(*[a[l] for a in params])
        xp = _block(xp, p_prompt[l], wts, tabs, **cfg)
        xs = _block(xs, p_sample[l], wts, tabs_s, **cfg)
    return (xp, xs)
```
